```python
import math
import jax
import jax.numpy as jnp
from jax import lax
import numpy as np

D_MODEL = 1024
BATCH = 4
SEQ = 4096
DEPTH = 2
DEC_BATCH = 8
DEC_SEQ = 8192
PAST_LEN = 128

GRID_W = 64
NA_HEADS = 16
NA_HEAD_DIM = D_MODEL // NA_HEADS
NA_WIN_ROWS = 8
NA_WIN_COLS = 16
NA_RPB_ROWS = 2 * NA_WIN_ROWS - 1
NA_RPB_COLS = 2 * NA_WIN_COLS - 1
DA_HEAD_DIM = 64
DA_HEADS = D_MODEL // (2 * DA_HEAD_DIM)
DA_SUBLN_EPS = 1e-5
Q_BLOCK = 128
ROPE_THETA = 10000.0
D_FF = 2816
CONV_W = 3
NORM_EPS = 1e-6

N_NA_LAYERS = (DEPTH + 1) // 2
N_DA_LAYERS = DEPTH // 2

kernel_name = 'hybrid_natten_diffattn_convglu_encoder'


def rms_norm(x, g, eps=NORM_EPS):
    xf = x.astype(jnp.float32)
    y = xf * lax.rsqrt(jnp.mean(xf * xf, axis=-1, keepdims=True) + eps)
    return (y * g.astype(jnp.float32)).astype(x.dtype)


def rotary_tables(T, dim):
    inv = 1.0 / (ROPE_THETA ** (jnp.arange(0, dim, 2, dtype=jnp.float32) / dim))
    ang = jnp.arange(T, dtype=jnp.float32)[:, None] * inv[None, :]
    ang = jnp.concatenate([ang, ang], axis=-1)
    return jnp.cos(ang), jnp.sin(ang)


def apply_rotary(x, cos, sin):
    half = x.shape[-1] // 2
    rot = jnp.concatenate([-x[..., half:], x[..., :half]], axis=-1)
    return (x.astype(jnp.float32) * cos + rot.astype(jnp.float32) * sin).astype(x.dtype)


def neighborhood_attention(x, w_qkv, rpb, w_o):
    B, T, _ = x.shape
    rows = T // GRID_W
    wr = min(NA_WIN_ROWS, rows)
    qkv = (x @ w_qkv).reshape(B, rows, GRID_W, 3, NA_HEADS, NA_HEAD_DIM)
    q = qkv[:, :, :, 0].transpose(1, 0, 3, 2, 4)
    k = qkv[:, :, :, 1].transpose(0, 3, 1, 2, 4)
    v = qkv[:, :, :, 2].transpose(0, 3, 1, 2, 4)
    cols = np.arange(GRID_W)
    col_start = np.clip(cols - NA_WIN_COLS // 2, 0, GRID_W - NA_WIN_COLS)
    col_idx = col_start[:, None] + np.arange(NA_WIN_COLS)[None, :]
    dc_idx = col_idx - cols[:, None] + (NA_WIN_COLS - 1)
    scale = NA_HEAD_DIM ** -0.5

    def row_block(args):
        q_r, r = args
        r0 = jnp.clip(r - wr // 2, 0, rows - wr)
        k_rows = lax.dynamic_slice_in_dim(k, r0, wr, axis=2)
        v_rows = lax.dynamic_slice_in_dim(v, r0, wr, axis=2)
        k_nb = k_rows[:, :, :, col_idx, :]
        v_nb = v_rows[:, :, :, col_idx, :]
        s = jnp.einsum('bhqd,bhrqcd->bhqrc', q_r, k_nb).astype(jnp.float32) * scale
        dr_idx = r0 + jnp.arange(wr) - r + (NA_WIN_ROWS - 1)
        bias = rpb[:, dr_idx][:, :, dc_idx]
        s = s + bias.transpose(0, 2, 1, 3).astype(jnp.float32)[None]
        p = jax.nn.softmax(s.reshape(B, NA_HEADS, GRID_W, wr * NA_WIN_COLS), axis=-1)
        p = p.reshape(B, NA_HEADS, GRID_W, wr, NA_WIN_COLS).astype(v.dtype)
        return jnp.einsum('bhqrc,bhrqcd->bhqd', p, v_nb)

    o = lax.map(row_block, (q, jnp.arange(rows)))
    o = o.transpose(1, 0, 3, 2, 4).reshape(B, T, D_MODEL)
    return o @ w_o


def diff_attention(x, w_q, w_k, w_v, lq1, lk1, lq2, lk2, subln_g, w_o, lambda_init):
    B, T, _ = x.shape
    q = (x @ w_q).reshape(B, T, DA_HEADS, 2, DA_HEAD_DIM)
    k = (x @ w_k).reshape(B, T, DA_HEADS, 2, DA_HEAD_DIM)
    v = (x @ w_v).reshape(B, T, DA_HEADS, 2 * DA_HEAD_DIM)
    cos, sin = rotary_tables(T, DA_HEAD_DIM)
    cos, sin = cos[:, None, None, :], sin[:, None, None, :]
    q = apply_rotary(q, cos, sin).transpose(0, 2, 3, 1, 4)
    k = apply_rotary(k, cos, sin).transpose(0, 2, 3, 1, 4)
    v = v.transpose(0, 2, 1, 3)
    lam = (jnp.exp(jnp.sum(lq1.astype(jnp.float32) * lk1.astype(jnp.float32)))
           - jnp.exp(jnp.sum(lq2.astype(jnp.float32) * lk2.astype(jnp.float32)))
           + lambda_init)
    scale = DA_HEAD_DIM ** -0.5
    nb = T // Q_BLOCK
    qb = q.reshape(B, DA_HEADS, 2, nb, Q_BLOCK, DA_HEAD_DIM).transpose(3, 0, 1, 2, 4, 5)

    def q_block(qi):
        s = jnp.einsum('bhiqd,bhikd->bhiqk', qi, k).astype(jnp.float32) * scale
        p = jax.nn.softmax(s, axis=-1)
        a = (p[:, :, 0] - lam * p[:, :, 1]).astype(v.dtype)
        return jnp.einsum('bhqk,bhkd->bhqd', a, v)

    o = lax.map(q_block, qb)
    o = rms_norm(o, subln_g, DA_SUBLN_EPS) * (1.0 - lambda_init)
    o = o.transpose(1, 0, 3, 2, 4).reshape(B, T, DA_HEADS * 2 * DA_HEAD_DIM)
    return o @ w_o


def conv_glu_ffn(x, w_in, conv_w, conv_b, w_out):
    h = x @ w_in
    hp = jnp.pad(h, ((0, 0), (1, 1), (0, 0)))
    h = hp[:, :-2] * conv_w[0] + hp[:, 1:-1] * conv_w[1] + hp[:, 2:] * conv_w[2] + conv_b
    gate, up = jnp.split(h, 2, axis=-1)
    return (jax.nn.gelu(gate, approximate=True) * up) @ w_out


def lambda_init_fn(layer_idx):
    return 0.8 - 0.6 * math.exp(-0.3 * layer_idx)


def trunk(x, attn_pre_g, attn_post_g, ffn_pre_g, ffn_post_g,
          na_w_qkv, na_rpb, na_w_o,
          da_w_q, da_w_k, da_w_v, da_lambda_q1, da_lambda_k1, da_lambda_q2, da_lambda_k2,
          da_subln_g, da_w_o,
          ffn_w_in, ffn_conv_w, ffn_conv_b, ffn_w_out):
    for i in range(DEPTH):
        j = i // 2
        h = rms_norm(x, attn_pre_g[i])
        if i % 2 == 0:
            m = neighborhood_attention(h, na_w_qkv[j], na_rpb[j], na_w_o[j])
        else:
            m = diff_attention(h, da_w_q[j], da_w_k[j], da_w_v[j],
                               da_lambda_q1[j], da_lambda_k1[j], da_lambda_q2[j], da_lambda_k2[j],
                               da_subln_g[j], da_w_o[j], lambda_init_fn(i))
        x = x + rms_norm(m, attn_post_g[i])
        h = rms_norm(x, ffn_pre_g[i])
        f = conv_glu_ffn(h, ffn_w_in[i], ffn_conv_w[i], ffn_conv_b[i], ffn_w_out[i])
        x = x + rms_norm(f, ffn_post_g[i])
    return x


def _normal(k, shape, scale):
    return jax.random.normal(k, shape, jnp.float32) * scale


def setup_inputs(seed: int = 0) -> dict:
    key = jax.random.key(seed)
    ks = jax.random.split(key, 24)
    D, F = D_MODEL, D_FF
    da_qk = 2 * DA_HEADS * DA_HEAD_DIM
    da_v = DA_HEADS * 2 * DA_HEAD_DIM
    return {
        'x_prompt': _normal(ks[0], (BATCH, SEQ, D), 1.0),
        'x_sample': _normal(ks[1], (DEC_BATCH, DEC_SEQ, D), 1.0),
        'attn_pre_g': 1.0 + _normal(ks[2], (DEPTH, D), 0.1),
        'attn_post_g': 1.0 + _normal(ks[3], (DEPTH, D), 0.1),
        'ffn_pre_g': 1.0 + _normal(ks[4], (DEPTH, D), 0.1),
        'ffn_post_g': 1.0 + _normal(ks[5], (DEPTH, D), 0.1),
        'na_w_qkv': _normal(ks[6], (N_NA_LAYERS, D, 3 * D), D ** -0.5),
        'na_rpb': _normal(ks[7], (N_NA_LAYERS, NA_HEADS, NA_RPB_ROWS, NA_RPB_COLS), 0.1),
        'na_w_o': _normal(ks[8], (N_NA_LAYERS, D, D), D ** -0.5),
        'da_w_q': _normal(ks[9], (N_DA_LAYERS, D, da_qk), D ** -0.5),
        'da_w_k': _normal(ks[10], (N_DA_LAYERS, D, da_qk), D ** -0.5),
        'da_w_v': _normal(ks[11], (N_DA_LAYERS, D, da_v), D ** -0.5),
        'da_lambda_q1': _normal(ks[12], (N_DA_LAYERS, DA_HEAD_DIM), 0.1),
        'da_lambda_k1': _normal(ks[13], (N_DA_LAYERS, DA_HEAD_DIM), 0.1),
        'da_lambda_q2': _normal(ks[14], (N_DA_LAYERS, DA_HEAD_DIM), 0.1),
        'da_lambda_k2': _normal(ks[15], (N_DA_LAYERS, DA_HEAD_DIM), 0.1),
        'da_subln_g': 1.0 + _normal(ks[16], (N_DA_LAYERS, 2 * DA_HEAD_DIM), 0.1),
        'da_w_o': _normal(ks[17], (N_DA_LAYERS, da_v, D), da_v ** -0.5),
        'ffn_w_in': _normal(ks[18], (DEPTH, D, 2 * F), D ** -0.5),
        'ffn_conv_w': _normal(ks[19], (DEPTH, CONV_W, 2 * F), CONV_W ** -0.5),
        'ffn_conv_b': _normal(ks[20], (DEPTH, 2 * F), 0.01),
        'ffn_w_out': _normal(ks[21], (DEPTH, F, D), F ** -0.5),
    }


def reference(x_prompt, x_sample, attn_pre_g, attn_post_g, ffn_pre_g, ffn_post_g,
              na_w_qkv, na_rpb, na_w_o,
              da_w_q, da_w_k, da_w_v, da_lambda_q1, da_lambda_k1, da_lambda_q2, da_lambda_k2,
              da_subln_g, da_w_o,
              ffn_w_in, ffn_conv_w, ffn_conv_b, ffn_w_out):
    y_prompt = trunk(x_prompt, attn_pre_g, attn_post_g, ffn_pre_g, ffn_post_g,
                     na_w_qkv, na_rpb, na_w_o,
                     da_w_q, da_w_k, da_w_v, da_lambda_q1, da_lambda_k1, da_lambda_q2, da_lambda_k2,
                     da_subln_g, da_w_o,
                     ffn_w_in, ffn_conv_w, ffn_conv_b, ffn_w_out)
    y_sample = trunk(x_sample, attn_pre_g, attn_post_g, ffn_pre_g, ffn_post_g,
                     na_w_qkv, na_rpb, na_w_o,
                     da_w_q, da_w_k, da_w_v, da_lambda_q1, da_lambda_k1, da_lambda_q2, da_lambda_k2,
                     da_subln_g, da_w_o,
                     ffn_w_in, ffn_conv_w, ffn_conv_b, ffn_w_out)
    return (y_prompt, y_sample)
```

```python
import functools
import math

import numpy as np
import jax
import jax.numpy as jnp
from jax import lax
from jax.experimental import pallas as pl
from jax.experimental.pallas import tpu as pltpu

D_MODEL = 1024
GRID_W = 64
NA_HEADS = 16
NA_HEAD_DIM = 64
NA_WIN_ROWS = 8
NA_WIN_COLS = 16
DA_HEADS = 8
DA_HEAD_DIM = 64
DA_SUBLN_EPS = 1e-5
ROPE_THETA = 10000.0
D_FF = 2816
NORM_EPS = 1e-6

F32 = jnp.float32
BF16 = jnp.bfloat16
LOG2E = 1.4426950408889634
MASK_VALUE = -1e30

VMEM_LIMIT_BYTES = 56 * 1024 * 1024

TOKEN_TILE = 512
NA_ROWS_PER_STEP = 8
NA_WINDOW_FETCH_ROWS = 16
DA_Q_TILE = 256
DA_K_TILE = 512
DA_ONES_ROWS = 16
FFN_CHUNK = 256
CONV_HALO = 8


def _params(*semantics):
    return pltpu.CompilerParams(dimension_semantics=semantics, vmem_limit_bytes=VMEM_LIMIT_BYTES)


def _rms(x, g, eps=NORM_EPS):
    ms = jnp.mean(x * x, axis=-1, keepdims=True)
    return x * lax.rsqrt(ms + eps) * g


def _resident(shape):
    nd = len(shape)
    return pl.BlockSpec(shape, lambda *_: (0,) * nd, pipeline_mode=pl.Buffered(1))


def _norm_qkv_na_kernel(x_ref, g_ref, w_ref, o_ref, *, q_scale):
    xn = _rms(x_ref[...], g_ref[...]).astype(BF16)
    for c in range(3):
        r = jnp.dot(xn, w_ref[:, c * D_MODEL:(c + 1) * D_MODEL], preferred_element_type=F32)
        if c == 0:
            r = r * q_scale
        o_ref[:, c * D_MODEL:(c + 1) * D_MODEL] = r.astype(BF16)


def _norm_qkv_na(x2d, g, w_bf16):
    n = x2d.shape[0]
    tm = TOKEN_TILE
    return pl.pallas_call(
        functools.partial(_norm_qkv_na_kernel, q_scale=NA_HEAD_DIM ** -0.5),
        out_shape=jax.ShapeDtypeStruct((n, 3 * D_MODEL), BF16),
        grid=(n // tm,),
        in_specs=[
            pl.BlockSpec((tm, D_MODEL), lambda i: (i, 0)),
            _resident((1, D_MODEL)),
            _resident((D_MODEL, 3 * D_MODEL)),
        ],
        out_specs=pl.BlockSpec((tm, 3 * D_MODEL), lambda i: (i, 0)),
        compiler_params=_params("parallel"),
        name="norm_qkv_na",
    )(x2d, g, w_bf16)


def _na_window_start(i, rows):
    return jnp.clip(i * NA_ROWS_PER_STEP - NA_WIN_ROWS // 2, 0, rows - NA_WINDOW_FETCH_ROWS)


def _na_kernel(q_ref, k_ref, v_ref, bias_ref, o_ref, *, rows):
    i = pl.program_id(1)
    w0 = _na_window_start(i, rows)
    lane = lax.broadcasted_iota(jnp.int32, (GRID_W, 2 * NA_HEAD_DIM), 1)
    first_head = lane < NA_HEAD_DIM
    n_keys = NA_WIN_ROWS * GRID_W

    def row_body(j, carry):
        r = i * NA_ROWS_PER_STEP + j
        r0 = jnp.clip(r - NA_WIN_ROWS // 2, 0, rows - NA_WIN_ROWS)
        case = r0 - r + (NA_WIN_ROWS - 1)
        off = pl.multiple_of((r0 - w0) * GRID_W, GRID_W)
        qoff = pl.multiple_of(j * GRID_W, GRID_W)
        for p in range(NA_HEADS // 2):
            cols = slice(p * 2 * NA_HEAD_DIM, (p + 1) * 2 * NA_HEAD_DIM)
            qp = q_ref[pl.ds(qoff, GRID_W), cols]
            zero = jnp.zeros_like(qp)
            qz = jnp.concatenate([jnp.where(first_head, qp, zero), jnp.where(first_head, zero, qp)], axis=0)
            kp = k_ref[0, pl.ds(off, n_keys), cols]
            vp = v_ref[0, pl.ds(off, n_keys), cols]
            s = lax.dot_general(qz, kp, (((1,), (1,)), ((), ())), preferred_element_type=F32)
            s = s + bias_ref[case, p]
            m = jnp.max(s, axis=-1, keepdims=True)
            e = jnp.exp(s - m)
            l = jnp.sum(e, axis=-1, keepdims=True)
            pv = jnp.dot(e.astype(BF16), vp, preferred_element_type=F32) / l
            o = jnp.where(first_head, pv[:GRID_W], pv[GRID_W:])
            o_ref[pl.ds(qoff, GRID_W), cols] = o.astype(BF16)
        return carry

    lax.fori_loop(0, NA_ROWS_PER_STEP, row_body, 0)


def _na_attention(qkv, bias):
    b, t, _ = qkv.shape
    rows = t // GRID_W
    tq = NA_ROWS_PER_STEP * GRID_W
    tw = NA_WINDOW_FETCH_ROWS * GRID_W
    assert rows % NA_ROWS_PER_STEP == 0 and rows >= NA_WINDOW_FETCH_ROWS

    def kv_spec(col):
        return pl.BlockSpec((pl.Element(1), pl.Element(tw), pl.Element(D_MODEL)),
                            lambda bi, i: (bi, _na_window_start(i, rows) * GRID_W, col * D_MODEL))

    return pl.pallas_call(
        functools.partial(_na_kernel, rows=rows),
        out_shape=jax.ShapeDtypeStruct((b, t, D_MODEL), BF16),
        grid=(b, rows // NA_ROWS_PER_STEP),
        in_specs=[
            pl.BlockSpec((None, tq, D_MODEL), lambda bi, i: (bi, i, 0)),
            kv_spec(1),
            kv_spec(2),
            _resident(bias.shape),
        ],
        out_specs=pl.BlockSpec((None, tq, D_MODEL), lambda bi, i: (bi, i, 0)),
        compiler_params=_params("parallel", "parallel"),
        name="na_attention",
    )(qkv, qkv, qkv, bias)


def _na_bias_table(rpb):
    cols = np.arange(GRID_W)
    col_start = np.clip(cols - NA_WIN_COLS // 2, 0, GRID_W - NA_WIN_COLS)
    rel = cols[None, :] - cols[:, None]
    valid = (cols[None, :] >= col_start[:, None]) & (cols[None, :] < col_start[:, None] + NA_WIN_COLS)
    dc = np.clip(rel + (NA_WIN_COLS - 1), 0, 2 * NA_WIN_COLS - 2)
    dr = np.arange(NA_WIN_ROWS)[:, None] + np.arange(NA_WIN_ROWS)[None, :]
    tab = rpb.astype(F32)[:, dr[:, :, None, None], dc[None, None, :, :]]
    tab = jnp.where(valid[None, None, None], tab, MASK_VALUE)
    tab = tab.transpose(1, 0, 3, 2, 4)
    return tab.reshape(NA_WIN_ROWS, NA_HEADS // 2, 2 * GRID_W, NA_WIN_ROWS * GRID_W)


def _proj_kernel(x_ref, a_ref, w_ref, g_ref, o_ref):
    m = jnp.dot(a_ref[...], w_ref[...], preferred_element_type=F32)
    o_ref[...] = x_ref[...] + _rms(m, g_ref[...])


def _proj_residual(x2d, a2d, w_bf16, g):
    n = x2d.shape[0]
    tm = TOKEN_TILE
    return pl.pallas_call(
        _proj_kernel,
        out_shape=jax.ShapeDtypeStruct((n, D_MODEL), F32),
        grid=(n // tm,),
        in_specs=[
            pl.BlockSpec((tm, D_MODEL), lambda i: (i, 0)),
            pl.BlockSpec((tm, D_MODEL), lambda i: (i, 0)),
            _resident((D_MODEL, D_MODEL)),
            _resident((1, D_MODEL)),
        ],
        out_specs=pl.BlockSpec((tm, D_MODEL), lambda i: (i, 0)),
        compiler_params=_params("parallel"),
        name="proj_residual",
    )(x2d, a2d, w_bf16, g)


def _gelu_tanh(x):
    return 0.5 * x * (1.0 + jnp.tanh(math.sqrt(2.0 / math.pi) * (x + 0.044715 * (x * x * x))))


def _ffn_kernel(x_ref, xprev_ref, xnext_ref, gpre_ref, win_ref, cw_ref, cb_ref, wout_ref, gpost_ref,
                o_ref, xcat_ref, xcb_ref, acc_ref, *, tm, n_chunks):
    i = pl.program_id(1)
    last = pl.num_programs(1) - 1
    h = CONV_HALO
    g = gpre_ref[...]
    x = x_ref[...]
    xcat_ref[0:h] = _rms(xprev_ref[...], g) * (i > 0).astype(F32)
    xcat_ref[h:tm + h] = _rms(x, g)
    xcat_ref[tm + h:tm + 2 * h] = _rms(xnext_ref[...], g) * (i < last).astype(F32)
    xcb_ref[...] = xcat_ref[...].astype(BF16)
    acc_ref[...] = jnp.zeros_like(acc_ref)

    def conv(hh, w, b):
        return hh[h - 1:tm + h - 1] * w[0:1] + hh[h:tm + h] * w[1:2] + hh[h + 1:tm + h + 1] * w[2:3] + b

    def chunk_body(c, carry):
        xc = xcb_ref[...]
        hg = jnp.dot(xc, win_ref[c], preferred_element_type=F32)
        hu = jnp.dot(xc, win_ref[n_chunks + c], preferred_element_type=F32)
        gate = conv(hg, cw_ref[c], cb_ref[c])
        up = conv(hu, cw_ref[n_chunks + c], cb_ref[n_chunks + c])
        act = (_gelu_tanh(gate) * up).astype(BF16)
        acc_ref[...] += jnp.dot(act, wout_ref[c], preferred_element_type=F32)
        return carry

    lax.fori_loop(0, n_chunks, chunk_body, 0)
    o_ref[...] = x + _rms(acc_ref[...], gpost_ref[...])


def _ffn(x, gpre, win_c, cw_c, cb_c, wout_c, gpost):
    b, t, _ = x.shape
    tm = TOKEN_TILE
    h = CONV_HALO
    n_chunks = D_FF // FFN_CHUNK
    halo_blocks = tm // h
    n_halo = t // h
    return pl.pallas_call(
        functools.partial(_ffn_kernel, tm=tm, n_chunks=n_chunks),
        out_shape=jax.ShapeDtypeStruct((b, t, D_MODEL), F32),
        grid=(b, t // tm),
        in_specs=[
            pl.BlockSpec((None, tm, D_MODEL), lambda bi, i: (bi, i, 0)),
            pl.BlockSpec((None, h, D_MODEL), lambda bi, i: (bi, jnp.maximum(i * halo_blocks - 1, 0), 0)),
            pl.BlockSpec((None, h, D_MODEL), lambda bi, i: (bi, jnp.minimum((i + 1) * halo_blocks, n_halo - 1), 0)),
            _resident((1, D_MODEL)),
            _resident(win_c.shape),
            _resident(cw_c.shape),
            _resident(cb_c.shape),
            _resident(wout_c.shape),
            _resident((1, D_MODEL)),
        ],
        out_specs=pl.BlockSpec((None, tm, D_MODEL), lambda bi, i: (bi, i, 0)),
        scratch_shapes=[
            pltpu.VMEM((tm + 2 * h, D_MODEL), F32),
            pltpu.VMEM((tm + 2 * h, D_MODEL), BF16),
            pltpu.VMEM((tm, D_MODEL), F32),
        ],
        compiler_params=_params("parallel", "parallel"),
        name="conv_glu_ffn",
    )(x, x, x, gpre, win_c, cw_c, cb_c, wout_c, gpost)


def _norm_qkv_da_kernel(x_ref, g_ref, wqT_ref, wk_ref, wvT_ref, cos_ref, sin_ref, cosT_ref, sinT_ref,
                        qT_ref, k_ref, vT_ref, *, q_scale):
    xn = _rms(x_ref[...], g_ref[...]).astype(BF16)
    nt = (((1,), (1,)), ((), ()))
    half = DA_HEAD_DIM // 2

    kk = jnp.dot(xn, wk_ref[...], preferred_element_type=F32)
    cos = cos_ref[...]
    sin = sin_ref[...]
    lane = lax.broadcasted_iota(jnp.int32, cos.shape, 1)
    low = (lane % DA_HEAD_DIM) < half
    for c in range(D_MODEL // 128):
        xc = kk[:, c * 128:(c + 1) * 128]
        rot = jnp.where(low, pltpu.roll(xc, 128 - half, 1), pltpu.roll(xc, half, 1))
        k_ref[:, c * 128:(c + 1) * 128] = (xc * cos + rot * sin).astype(BF16)

    qT = lax.dot_general(wqT_ref[...], xn, nt, preferred_element_type=F32)
    cl, ch = cosT_ref[0:half], cosT_ref[half:DA_HEAD_DIM]
    sl, sh = sinT_ref[0:half], sinT_ref[half:DA_HEAD_DIM]
    for grp in range(D_MODEL // DA_HEAD_DIM):
        base = grp * DA_HEAD_DIM
        lo = qT[base:base + half]
        hi = qT[base + half:base + DA_HEAD_DIM]
        qT_ref[base:base + half] = ((lo * cl - hi * sl) * q_scale).astype(BF16)
        qT_ref[base + half:base + DA_HEAD_DIM] = ((hi * ch + lo * sh) * q_scale).astype(BF16)

    vT = lax.dot_general(wvT_ref[...], xn, nt, preferred_element_type=F32)
    for hd in range(DA_HEADS):
        vT_ref[hd] = vT[hd * 2 * DA_HEAD_DIM:(hd + 1) * 2 * DA_HEAD_DIM].astype(BF16)


def _norm_qkv_da(x, g, wqT, wk, wvT, tables):
    b, t, _ = x.shape
    tm = DA_K_TILE
    nk = t // tm
    cos128, sin128, cosT, sinT = tables
    hv = 2 * DA_HEAD_DIM
    return pl.pallas_call(
        functools.partial(_norm_qkv_da_kernel, q_scale=(DA_HEAD_DIM ** -0.5) * LOG2E),
        out_shape=(
            jax.ShapeDtypeStruct((b, D_MODEL, t), BF16),
            jax.ShapeDtypeStruct((b, t, D_MODEL), BF16),
            jax.ShapeDtypeStruct((b, DA_HEADS, nk, hv, tm), BF16),
        ),
        grid=(b, nk),
        in_specs=[
            pl.BlockSpec((None, tm, D_MODEL), lambda bi, i: (bi, i, 0)),
            _resident((1, D_MODEL)),
            _resident((D_MODEL, D_MODEL)),
            _resident((D_MODEL, D_MODEL)),
            _resident((D_MODEL, D_MODEL)),
            pl.BlockSpec((tm, 128), lambda bi, i: (i, 0)),
            pl.BlockSpec((tm, 128), lambda bi, i: (i, 0)),
            pl.BlockSpec((DA_HEAD_DIM, tm), lambda bi, i: (0, i)),
            pl.BlockSpec((DA_HEAD_DIM, tm), lambda bi, i: (0, i)),
        ],
        out_specs=(
            pl.BlockSpec((None, D_MODEL, tm), lambda bi, i: (bi, 0, i)),
            pl.BlockSpec((None, tm, D_MODEL), lambda bi, i: (bi, i, 0)),
            pl.BlockSpec((None, DA_HEADS, None, hv, tm), lambda bi, i: (bi, 0, i, 0, 0)),
        ),
        compiler_params=_params("parallel", "parallel"),
        name="norm_qkv_da",
    )(x, g, wqT, wk, wvT, cos128, sin128, cosT, sinT)


def _rotary_tables(t):
    half = DA_HEAD_DIM // 2
    inv = 1.0 / (ROPE_THETA ** (jnp.arange(0, DA_HEAD_DIM, 2, dtype=F32) / DA_HEAD_DIM))
    ang = jnp.arange(t, dtype=F32)[:, None] * inv[None, :]
    ang = jnp.concatenate([ang, ang], axis=-1)
    cos, sin = jnp.cos(ang), jnp.sin(ang)
    sign = jnp.where(jnp.arange(DA_HEAD_DIM) < half, -1.0, 1.0).astype(F32)
    cos128 = jnp.concatenate([cos, cos], axis=-1)
    sin128 = jnp.concatenate([sin * sign, sin * sign], axis=-1)
    return cos128, sin128, cos.T, sin.T


def _da_kernel(qT_ref, k_ref, vT_ref, lam_ref, g_ref, o_ref, qz_ref, m_ref, acc_ref,
               *, tq, tk, nk, lambda_init):
    hd = DA_HEAD_DIM
    hv = 2 * DA_HEAD_DIM
    qT = qT_ref[...]
    zero = jnp.zeros((hd, tq), BF16)
    qz_ref[0:hd, 0:tq] = qT[0:hd]
    qz_ref[hd:hv, 0:tq] = zero
    qz_ref[0:hd, tq:2 * tq] = zero
    qz_ref[hd:hv, tq:2 * tq] = qT[hd:hv]
    m_ref[...] = jnp.full(m_ref.shape, MASK_VALUE, F32)
    acc_ref[...] = jnp.zeros_like(acc_ref)
    ones = jnp.ones((DA_ONES_ROWS, tk), BF16)

    def k_body(j, carry):
        kj = k_ref[pl.ds(pl.multiple_of(j * tk, tk), tk), :]
        st = jnp.dot(kj, qz_ref[...], preferred_element_type=F32)
        m_old = m_ref[...]
        m_new = jnp.maximum(m_old, jnp.max(st, axis=0, keepdims=True))
        alpha = jnp.exp2(m_old - m_new)
        p = jnp.exp2(st - m_new).astype(BF16)
        vj = jnp.concatenate([vT_ref[j], ones], axis=0)
        acc_ref[...] = acc_ref[...] * alpha + jnp.dot(vj, p, preferred_element_type=F32)
        m_ref[...] = m_new
        return carry

    lax.fori_loop(0, nk, k_body, 0)

    acc = acc_ref[...]
    o1 = acc[0:hv, 0:tq] / acc[hv:hv + 1, 0:tq]
    o2 = acc[0:hv, tq:2 * tq] / acc[hv:hv + 1, tq:2 * tq]
    lv = lam_ref[...]
    lam = (jnp.exp(jnp.sum(lv[0:1] * lv[1:2], axis=-1, keepdims=True))
           - jnp.exp(jnp.sum(lv[2:3] * lv[3:4], axis=-1, keepdims=True)) + lambda_init)
    o = o1 - lam * o2
    ms = jnp.mean(o * o, axis=0, keepdims=True)
    y = o * lax.rsqrt(ms + DA_SUBLN_EPS) * g_ref[...] * (1.0 - lambda_init)
    o_ref[...] = y.T.astype(BF16)


def _da_attention(qT, k, vT, lam_vecs, subln_g_col, lambda_init):
    b, t, _ = k.shape
    tq, tk = DA_Q_TILE, DA_K_TILE
    nk = t // tk
    hv = 2 * DA_HEAD_DIM
    return pl.pallas_call(
        functools.partial(_da_kernel, tq=tq, tk=tk, nk=nk, lambda_init=lambda_init),
        out_shape=jax.ShapeDtypeStruct((b, t, D_MODEL), BF16),
        grid=(b, DA_HEADS, t // tq),
        in_specs=[
            pl.BlockSpec((None, hv, tq), lambda bi, h, qi: (bi, h, qi)),
            pl.BlockSpec((None, t, hv), lambda bi, h, qi: (bi, 0, h)),
            pl.BlockSpec((None, None, nk, hv, tk), lambda bi, h, qi: (bi, h, 0, 0, 0)),
            _resident(lam_vecs.shape),
            _resident(subln_g_col.shape),
        ],
        out_specs=pl.BlockSpec((None, tq, hv), lambda bi, h, qi: (bi, qi, h)),
        scratch_shapes=[
            pltpu.VMEM((hv, 2 * tq), BF16),
            pltpu.VMEM((1, 2 * tq), F32),
            pltpu.VMEM((hv + DA_ONES_ROWS, 2 * tq), F32),
        ],
        compiler_params=_params("parallel", "parallel", "arbitrary"),
        name="diff_attention",
    )(qT, k, vT, lam_vecs, subln_g_col)


def _lambda_init(layer_idx):
    return 0.8 - 0.6 * math.exp(-0.3 * layer_idx)


def _trunk(x, p):
    b, t, d = x.shape
    n = b * t
    rot = _rotary_tables(t)
    for i in range(2):
        if i == 0:
            qkv = _norm_qkv_na(x.reshape(n, d), p["attn_pre_g"][i], p["na_w_qkv"])
            a = _na_attention(qkv.reshape(b, t, 3 * d), p["na_bias"])
            w_o = p["na_w_o"]
        else:
            qT, k, vT = _norm_qkv_da(x, p["attn_pre_g"][i], p["da_wqT"], p["da_wk"], p["da_wvT"], rot)
            a = _da_attention(qT, k, vT, p["da_lam"], p["da_subln_g"], _lambda_init(i))
            w_o = p["da_w_o"]
        x = _proj_residual(x.reshape(n, d), a.reshape(n, d), w_o, p["attn_post_g"][i]).reshape(b, t, d)
        x = _ffn(x, p["ffn_pre_g"][i], p["ffn_w_in"][i], p["ffn_conv_w"][i], p["ffn_conv_b"][i],
                 p["ffn_w_out"][i], p["ffn_post_g"][i])
    return x


def _prepare(attn_pre_g, attn_post_g, ffn_pre_g, ffn_post_g, na_w_qkv, na_rpb, na_w_o,
             da_w_q, da_w_k, da_w_v, da_lambda_q1, da_lambda_k1, da_lambda_q2, da_lambda_k2,
             da_subln_g, da_w_o, ffn_w_in, ffn_conv_w, ffn_conv_b, ffn_w_out):
    depth = ffn_w_in.shape[0]
    nc = D_FF // FFN_CHUNK

    def chunk_cols(w):
        lead = w.shape[:-1]
        w = w.reshape(lead + (2 * nc, FFN_CHUNK))
        return jnp.moveaxis(w, -2, 0)

    row = lambda v: v.reshape(v.shape[0], 1, v.shape[1]).astype(F32)
    return {
        "attn_pre_g": row(attn_pre_g), "attn_post_g": row(attn_post_g),
        "ffn_pre_g": row(ffn_pre_g), "ffn_post_g": row(ffn_post_g),
        "na_w_qkv": na_w_qkv[0].astype(BF16),
        "na_bias": _na_bias_table(na_rpb[0]),
        "na_w_o": na_w_o[0].astype(BF16),
        "da_wqT": da_w_q[0].T.astype(BF16),
        "da_wk": da_w_k[0].astype(BF16),
        "da_wvT": da_w_v[0].T.astype(BF16),
        "da_lam": jnp.stack([da_lambda_q1[0], da_lambda_k1[0], da_lambda_q2[0], da_lambda_k2[0]]).astype(F32),
        "da_subln_g": da_subln_g[0].astype(F32).reshape(2 * DA_HEAD_DIM, 1),
        "da_w_o": da_w_o[0].astype(BF16),
        "ffn_w_in": [chunk_cols(ffn_w_in[i]).astype(BF16) for i in range(depth)],
        "ffn_conv_w": [chunk_cols(ffn_conv_w[i]).astype(F32) for i in range(depth)],
        "ffn_conv_b": [chunk_cols(ffn_conv_b[i][None]).astype(F32) for i in range(depth)],
        "ffn_w_out": [ffn_w_out[i].reshape(nc, FFN_CHUNK, D_MODEL).astype(BF16) for i in range(depth)],
    }


def kernel(x_prompt, x_sample, attn_pre_g, attn_post_g, ffn_pre_g, ffn_post_g, na_w_qkv, na_rpb, na_w_o,
           da_w_q, da_w_k, da_w_v, da_lambda_q1, da_lambda_k1, da_lambda_q2, da_lambda_k2, da_subln_g, da_w_o,
           ffn_w_in, ffn_conv_w, ffn_conv_b, ffn_w_out):
    p = _prepare(attn_pre_g, attn_post_g, ffn_pre_g, ffn_post_g, na_w_qkv, na_rpb, na_w_o,
                 da_w_q, da_w_k, da_w_v, da_lambda_q1, da_lambda_k1, da_lambda_q2, da_lambda_k2,
                 da_subln_g, da_w_o, ffn_w_in, ffn_conv_w, ffn_conv_b, ffn_w_out)
    return (_trunk(x_prompt, p), _trunk(x_sample, p))
```

```python
import functools
import math

import numpy as np
import jax
import jax.numpy as jnp
from jax import lax
from jax.experimental import pallas as pl
from jax.experimental.pallas import tpu as pltpu

D_MODEL = 1024
GRID_W = 64
NA_HEADS = 16
NA_HEAD_DIM = 64
NA_WIN_ROWS = 8
NA_WIN_COLS = 16
DA_HEADS = 8
DA_HEAD_DIM = 64
DA_SUBLN_EPS = 1e-5
ROPE_THETA = 10000.0
D_FF = 2816
NORM_EPS = 1e-6

F32 = jnp.float32
BF16 = jnp.bfloat16
LOG2E = 1.4426950408889634
MASK_VALUE = -1e30

VMEM_LIMIT_BYTES = 56 * 1024 * 1024

TOKEN_TILE = 512
NA_ROWS_PER_STEP = 8
NA_WINDOW_FETCH_ROWS = 16
DA_Q_TILE = 256
DA_K_TILE = 512
DA_ONES_ROWS = 16
FFN_CHUNK = 256
CONV_HALO = 8


def _params(*semantics):
    return pltpu.CompilerParams(dimension_semantics=semantics, vmem_limit_bytes=VMEM_LIMIT_BYTES)


def _rms(x, g, eps=NORM_EPS):
    ms = jnp.mean(x * x, axis=-1, keepdims=True)
    return x * lax.rsqrt(ms + eps) * g


def _resident(shape):
    nd = len(shape)
    return pl.BlockSpec(shape, lambda *_: (0,) * nd, pipeline_mode=pl.Buffered(1))


def _norm_qkv_na_kernel(x_ref, g_ref, w_ref, o_ref, *, q_scale):
    xn = _rms(x_ref[...], g_ref[...]).astype(BF16)
    for c in range(3):
        r = jnp.dot(xn, w_ref[:, c * D_MODEL:(c + 1) * D_MODEL], preferred_element_type=F32)
        if c == 0:
            r = r * q_scale
        o_ref[:, c * D_MODEL:(c + 1) * D_MODEL] = r.astype(BF16)


def _norm_qkv_na(x2d, g, w_bf16):
    n = x2d.shape[0]
    tm = TOKEN_TILE
    return pl.pallas_call(
        functools.partial(_norm_qkv_na_kernel, q_scale=NA_HEAD_DIM ** -0.5),
        out_shape=jax.ShapeDtypeStruct((n, 3 * D_MODEL), BF16),
        grid=(n // tm,),
        in_specs=[
            pl.BlockSpec((tm, D_MODEL), lambda i: (i, 0)),
            _resident((1, D_MODEL)),
            _resident((D_MODEL, 3 * D_MODEL)),
        ],
        out_specs=pl.BlockSpec((tm, 3 * D_MODEL), lambda i: (i, 0)),
        compiler_params=_params("parallel"),
        name="norm_qkv_na",
    )(x2d, g, w_bf16)


def _na_window_start(i, rows):
    return jnp.clip(i * NA_ROWS_PER_STEP - NA_WIN_ROWS // 2, 0, rows - NA_WINDOW_FETCH_ROWS)


def _na_kernel(q_ref, k_ref, v_ref, bias_ref, o_ref, *, rows):
    i = pl.program_id(1)
    w0 = _na_window_start(i, rows)
    lane = lax.broadcasted_iota(jnp.int32, (GRID_W, 2 * NA_HEAD_DIM), 1)
    first_head = lane < NA_HEAD_DIM
    n_keys = NA_WIN_ROWS * GRID_W

    def row_body(j, carry):
        r = i * NA_ROWS_PER_STEP + j
        r0 = jnp.clip(r - NA_WIN_ROWS // 2, 0, rows - NA_WIN_ROWS)
        case = r0 - r + (NA_WIN_ROWS - 1)
        off = pl.multiple_of((r0 - w0) * GRID_W, GRID_W)
        qoff = pl.multiple_of(j * GRID_W, GRID_W)
        for p in range(NA_HEADS // 2):
            cols = slice(p * 2 * NA_HEAD_DIM, (p + 1) * 2 * NA_HEAD_DIM)
            qp = q_ref[pl.ds(qoff, GRID_W), cols]
            zero = jnp.zeros_like(qp)
            qz = jnp.concatenate([jnp.where(first_head, qp, zero), jnp.where(first_head, zero, qp)], axis=0)
            kp = k_ref[0, pl.ds(off, n_keys), cols]
            vp = v_ref[0, pl.ds(off, n_keys), cols]
            s = lax.dot_general(qz, kp, (((1,), (1,)), ((), ())), preferred_element_type=F32)
            s = s + bias_ref[case, p]
            m = jnp.max(s, axis=-1, keepdims=True)
            e = jnp.exp(s - m)
            l = jnp.sum(e, axis=-1, keepdims=True)
            pv = jnp.dot(e.astype(BF16), vp, preferred_element_type=F32) / l
            o = jnp.where(first_head, pv[:GRID_W], pv[GRID_W:])
            o_ref[pl.ds(qoff, GRID_W), cols] = o.astype(BF16)
        return carry

    lax.fori_loop(0, NA_ROWS_PER_STEP, row_body, 0)


def _na_attention(qkv, bias):
    b, t, _ = qkv.shape
    rows = t // GRID_W
    tq = NA_ROWS_PER_STEP * GRID_W
    tw = NA_WINDOW_FETCH_ROWS * GRID_W
    assert rows % NA_ROWS_PER_STEP == 0 and rows >= NA_WINDOW_FETCH_ROWS

    def kv_spec(col):
        return pl.BlockSpec((pl.Element(1), pl.Element(tw), pl.Element(D_MODEL)),
                            lambda bi, i: (bi, _na_window_start(i, rows) * GRID_W, col * D_MODEL))

    return pl.pallas_call(
        functools.partial(_na_kernel, rows=rows),
        out_shape=jax.ShapeDtypeStruct((b, t, D_MODEL), BF16),
        grid=(b, rows // NA_ROWS_PER_STEP),
        in_specs=[
            pl.BlockSpec((None, tq, D_MODEL), lambda bi, i: (bi, i, 0)),
            kv_spec(1),
            kv_spec(2),
            _resident(bias.shape),
        ],
        out_specs=pl.BlockSpec((None, tq, D_MODEL), lambda bi, i: (bi, i, 0)),
        compiler_params=_params("parallel", "parallel"),
        name="na_attention",
    )(qkv, qkv, qkv, bias)


def _na_bias_table(rpb):
    cols = np.arange(GRID_W)
    col_start = np.clip(cols - NA_WIN_COLS // 2, 0, GRID_W - NA_WIN_COLS)
    valid = (cols[None, :] >= col_start[:, None]) & (cols[None, :] < col_start[:, None] + NA_WIN_COLS)
    pad = GRID_W - NA_WIN_COLS
    rp = jnp.pad(rpb.astype(F32), ((0, 0), (0, 0), (pad, pad)))
    u = jnp.stack([rp[:, :, GRID_W - 1 - c:2 * GRID_W - 1 - c] for c in range(GRID_W)], axis=2)
    u = jnp.where(valid[None, None], u, MASK_VALUE)
    tab = jnp.stack([u[:, s:s + NA_WIN_ROWS] for s in range(NA_WIN_ROWS)], axis=0)
    tab = tab.transpose(0, 1, 3, 2, 4)
    return tab.reshape(NA_WIN_ROWS, NA_HEADS // 2, 2 * GRID_W, NA_WIN_ROWS * GRID_W)


def _proj_kernel(x_ref, a_ref, w_ref, g_ref, o_ref):
    m = jnp.dot(a_ref[...], w_ref[...], preferred_element_type=F32)
    o_ref[...] = x_ref[...] + _rms(m, g_ref[...])


def _proj_residual(x2d, a2d, w_bf16, g):
    n = x2d.shape[0]
    tm = TOKEN_TILE
    return pl.pallas_call(
        _proj_kernel,
        out_shape=jax.ShapeDtypeStruct((n, D_MODEL), F32),
        grid=(n // tm,),
        in_specs=[
            pl.BlockSpec((tm, D_MODEL), lambda i: (i, 0)),
            pl.BlockSpec((tm, D_MODEL), lambda i: (i, 0)),
            _resident((D_MODEL, D_MODEL)),
            _resident((1, D_MODEL)),
        ],
        out_specs=pl.BlockSpec((tm, D_MODEL), lambda i: (i, 0)),
        compiler_params=_params("parallel"),
        name="proj_residual",
    )(x2d, a2d, w_bf16, g)


def _gelu_tanh(x):
    return 0.5 * x * (1.0 + jnp.tanh(math.sqrt(2.0 / math.pi) * (x + 0.044715 * (x * x * x))))


def _ffn_kernel(x_ref, xprev_ref, xnext_ref, gpre_ref, win_ref, cw_ref, cb_ref, wout_ref, gpost_ref,
                o_ref, xcat_ref, xcb_ref, acc_ref, *, tm, n_chunks):
    i = pl.program_id(1)
    last = pl.num_programs(1) - 1
    h = CONV_HALO
    g = gpre_ref[...]
    x = x_ref[...]
    xcat_ref[0:h] = _rms(xprev_ref[...], g) * (i > 0).astype(F32)
    xcat_ref[h:tm + h] = _rms(x, g)
    xcat_ref[tm + h:tm + 2 * h] = _rms(xnext_ref[...], g) * (i < last).astype(F32)
    xcb_ref[...] = xcat_ref[...].astype(BF16)
    acc_ref[...] = jnp.zeros_like(acc_ref)

    def conv(hh, w, b):
        return hh[h - 1:tm + h - 1] * w[0:1] + hh[h:tm + h] * w[1:2] + hh[h + 1:tm + h + 1] * w[2:3] + b

    def chunk_body(c, carry):
        xc = xcb_ref[...]
        hg = jnp.dot(xc, win_ref[c], preferred_element_type=F32)
        hu = jnp.dot(xc, win_ref[n_chunks + c], preferred_element_type=F32)
        gate = conv(hg, cw_ref[c], cb_ref[c])
        up = conv(hu, cw_ref[n_chunks + c], cb_ref[n_chunks + c])
        act = (_gelu_tanh(gate) * up).astype(BF16)
        acc_ref[...] += jnp.dot(act, wout_ref[c], preferred_element_type=F32)
        return carry

    lax.fori_loop(0, n_chunks, chunk_body, 0)
    o_ref[...] = x + _rms(acc_ref[...], gpost_ref[...])


def _ffn(x, gpre, win_c, cw_c, cb_c, wout_c, gpost):
    b, t, _ = x.shape
    tm = TOKEN_TILE
    h = CONV_HALO
    n_chunks = D_FF // FFN_CHUNK
    halo_blocks = tm // h
    n_halo = t // h
    return pl.pallas_call(
        functools.partial(_ffn_kernel, tm=tm, n_chunks=n_chunks),
        out_shape=jax.ShapeDtypeStruct((b, t, D_MODEL), F32),
        grid=(b, t // tm),
        in_specs=[
            pl.BlockSpec((None, tm, D_MODEL), lambda bi, i: (bi, i, 0)),
            pl.BlockSpec((None, h, D_MODEL), lambda bi, i: (bi, jnp.maximum(i * halo_blocks - 1, 0), 0)),
            pl.BlockSpec((None, h, D_MODEL), lambda bi, i: (bi, jnp.minimum((i + 1) * halo_blocks, n_halo - 1), 0)),
            _resident((1, D_MODEL)),
            _resident(win_c.shape),
            _resident(cw_c.shape),
            _resident(cb_c.shape),
            _resident(wout_c.shape),
            _resident((1, D_MODEL)),
        ],
        out_specs=pl.BlockSpec((None, tm, D_MODEL), lambda bi, i: (bi, i, 0)),
        scratch_shapes=[
            pltpu.VMEM((tm + 2 * h, D_MODEL), F32),
            pltpu.VMEM((tm + 2 * h, D_MODEL), BF16),
            pltpu.VMEM((tm, D_MODEL), F32),
        ],
        compiler_params=_params("parallel", "parallel"),
        name="conv_glu_ffn",
    )(x, x, x, gpre, win_c, cw_c, cb_c, wout_c, gpost)


def _norm_qkv_da_kernel(x_ref, g_ref, wqT_ref, wk_ref, wvT_ref, cos_ref, sin_ref, cosT_ref, sinT_ref,
                        qT_ref, k_ref, vT_ref, *, q_scale):
    xn = _rms(x_ref[...], g_ref[...]).astype(BF16)
    nt = (((1,), (1,)), ((), ()))
    half = DA_HEAD_DIM // 2

    kk = jnp.dot(xn, wk_ref[...], preferred_element_type=F32)
    cos = cos_ref[...]
    sin = sin_ref[...]
    lane = lax.broadcasted_iota(jnp.int32, cos.shape, 1)
    low = (lane % DA_HEAD_DIM) < half
    for c in range(D_MODEL // 128):
        xc = kk[:, c * 128:(c + 1) * 128]
        rot = jnp.where(low, pltpu.roll(xc, 128 - half, 1), pltpu.roll(xc, half, 1))
        k_ref[:, c * 128:(c + 1) * 128] = (xc * cos + rot * sin).astype(BF16)

    qT = lax.dot_general(wqT_ref[...], xn, nt, preferred_element_type=F32)
    cl, ch = cosT_ref[0:half], cosT_ref[half:DA_HEAD_DIM]
    sl, sh = sinT_ref[0:half], sinT_ref[half:DA_HEAD_DIM]
    for grp in range(D_MODEL // DA_HEAD_DIM):
        base = grp * DA_HEAD_DIM
        lo = qT[base:base + half]
        hi = qT[base + half:base + DA_HEAD_DIM]
        qT_ref[base:base + half] = ((lo * cl - hi * sl) * q_scale).astype(BF16)
        qT_ref[base + half:base + DA_HEAD_DIM] = ((hi * ch + lo * sh) * q_scale).astype(BF16)

    vT = lax.dot_general(wvT_ref[...], xn, nt, preferred_element_type=F32)
    for hd in range(DA_HEADS):
        vT_ref[hd] = vT[hd * 2 * DA_HEAD_DIM:(hd + 1) * 2 * DA_HEAD_DIM].astype(BF16)


def _norm_qkv_da(x, g, wqT, wk, wvT, tables):
    b, t, _ = x.shape
    tm = DA_K_TILE
    nk = t // tm
    cos128, sin128, cosT, sinT = tables
    hv = 2 * DA_HEAD_DIM
    return pl.pallas_call(
        functools.partial(_norm_qkv_da_kernel, q_scale=(DA_HEAD_DIM ** -0.5) * LOG2E),
        out_shape=(
            jax.ShapeDtypeStruct((b, D_MODEL, t), BF16),
            jax.ShapeDtypeStruct((b, t, D_MODEL), BF16),
            jax.ShapeDtypeStruct((b, DA_HEADS, nk, hv, tm), BF16),
        ),
        grid=(b, nk),
        in_specs=[
            pl.BlockSpec((None, tm, D_MODEL), lambda bi, i: (bi, i, 0)),
            _resident((1, D_MODEL)),
            _resident((D_MODEL, D_MODEL)),
            _resident((D_MODEL, D_MODEL)),
            _resident((D_MODEL, D_MODEL)),
            pl.BlockSpec((tm, 128), lambda bi, i: (i, 0)),
            pl.BlockSpec((tm, 128), lambda bi, i: (i, 0)),
            pl.BlockSpec((DA_HEAD_DIM, tm), lambda bi, i: (0, i)),
            pl.BlockSpec((DA_HEAD_DIM, tm), lambda bi, i: (0, i)),
        ],
        out_specs=(
            pl.BlockSpec((None, D_MODEL, tm), lambda bi, i: (bi, 0, i)),
            pl.BlockSpec((None, tm, D_MODEL), lambda bi, i: (bi, i, 0)),
            pl.BlockSpec((None, DA_HEADS, None, hv, tm), lambda bi, i: (bi, 0, i, 0, 0)),
        ),
        compiler_params=_params("parallel", "parallel"),
        name="norm_qkv_da",
    )(x, g, wqT, wk, wvT, cos128, sin128, cosT, sinT)


def _rotary_tables(t):
    half = DA_HEAD_DIM // 2
    inv = 1.0 / (ROPE_THETA ** (jnp.arange(0, DA_HEAD_DIM, 2, dtype=F32) / DA_HEAD_DIM))
    ang = jnp.arange(t, dtype=F32)[:, None] * inv[None, :]
    ang = jnp.concatenate([ang, ang], axis=-1)
    cos, sin = jnp.cos(ang), jnp.sin(ang)
    sign = jnp.where(jnp.arange(DA_HEAD_DIM) < half, -1.0, 1.0).astype(F32)
    cos128 = jnp.concatenate([cos, cos], axis=-1)
    sin128 = jnp.concatenate([sin * sign, sin * sign], axis=-1)
    return cos128, sin128, cos.T, sin.T


def _da_kernel(qT_ref, k_ref, vT_ref, lam_ref, g_ref, o_ref, qz_ref, st_a, st_b, mx_a, mx_b, m_ref, acc_ref,
               *, tq, tk, nk, lambda_init):
    hd = DA_HEAD_DIM
    hv = 2 * DA_HEAD_DIM
    qT = qT_ref[...]
    zero = jnp.zeros((hd, tq), BF16)
    qz_ref[0:hd, 0:tq] = qT[0:hd]
    qz_ref[hd:hv, 0:tq] = zero
    qz_ref[0:hd, tq:2 * tq] = zero
    qz_ref[hd:hv, tq:2 * tq] = qT[hd:hv]
    m_ref[...] = jnp.full(m_ref.shape, MASK_VALUE, F32)
    acc_ref[...] = jnp.zeros_like(acc_ref)
    ones = jnp.ones((DA_ONES_ROWS, tk), BF16)

    def scores(j, st_ref, mx_ref):
        kj = k_ref[pl.ds(pl.multiple_of(j * tk, tk), tk), :]
        st = jnp.dot(kj, qz_ref[...], preferred_element_type=F32)
        st_ref[...] = st
        mx_ref[...] = jnp.max(st, axis=0, keepdims=True)

    def accumulate(j, st_ref, mx_ref):
        m_old = m_ref[...]
        m_new = jnp.maximum(m_old, mx_ref[...])
        alpha = jnp.exp2(m_old - m_new)
        p = jnp.exp2(st_ref[...] - m_new).astype(BF16)
        vj = jnp.concatenate([vT_ref[j], ones], axis=0)
        acc_ref[...] = acc_ref[...] * alpha + jnp.dot(vj, p, preferred_element_type=F32)
        m_ref[...] = m_new

    scores(0, st_a, mx_a)

    def pair_body(i, carry):
        j = 2 * i
        scores(j + 1, st_b, mx_b)
        accumulate(j, st_a, mx_a)
        scores(j + 2, st_a, mx_a)
        accumulate(j + 1, st_b, mx_b)
        return carry

    lax.fori_loop(0, nk // 2 - 1, pair_body, 0)
    scores(nk - 1, st_b, mx_b)
    accumulate(nk - 2, st_a, mx_a)
    accumulate(nk - 1, st_b, mx_b)

    acc = acc_ref[...]
    o1 = acc[0:hv, 0:tq] / acc[hv:hv + 1, 0:tq]
    o2 = acc[0:hv, tq:2 * tq] / acc[hv:hv + 1, tq:2 * tq]
    lv = lam_ref[...]
    lam = (jnp.exp(jnp.sum(lv[0:1] * lv[1:2], axis=-1, keepdims=True))
           - jnp.exp(jnp.sum(lv[2:3] * lv[3:4], axis=-1, keepdims=True)) + lambda_init)
    o = o1 - lam * o2
    ms = jnp.mean(o * o, axis=0, keepdims=True)
    y = o * lax.rsqrt(ms + DA_SUBLN_EPS) * g_ref[...] * (1.0 - lambda_init)
    o_ref[...] = y.T.astype(BF16)


def _da_attention(qT, k, vT, lam_vecs, subln_g_col, lambda_init):
    b, t, _ = k.shape
    tq, tk = DA_Q_TILE, DA_K_TILE
    nk = t // tk
    hv = 2 * DA_HEAD_DIM
    assert nk >= 2 and nk % 2 == 0
    return pl.pallas_call(
        functools.partial(_da_kernel, tq=tq, tk=tk, nk=nk, lambda_init=lambda_init),
        out_shape=jax.ShapeDtypeStruct((b, t, D_MODEL), BF16),
        grid=(b, DA_HEADS, t // tq),
        in_specs=[
            pl.BlockSpec((None, hv, tq), lambda bi, h, qi: (bi, h, qi)),
            pl.BlockSpec((None, t, hv), lambda bi, h, qi: (bi, 0, h)),
            pl.BlockSpec((None, None, nk, hv, tk), lambda bi, h, qi: (bi, h, 0, 0, 0)),
            _resident(lam_vecs.shape),
            _resident(subln_g_col.shape),
        ],
        out_specs=pl.BlockSpec((None, tq, hv), lambda bi, h, qi: (bi, qi, h)),
        scratch_shapes=[
            pltpu.VMEM((hv, 2 * tq), BF16),
            pltpu.VMEM((tk, 2 * tq), F32),
            pltpu.VMEM((tk, 2 * tq), F32),
            pltpu.VMEM((1, 2 * tq), F32),
            pltpu.VMEM((1, 2 * tq), F32),
            pltpu.VMEM((1, 2 * tq), F32),
            pltpu.VMEM((hv + DA_ONES_ROWS, 2 * tq), F32),
        ],
        compiler_params=_params("parallel", "parallel", "arbitrary"),
        name="diff_attention",
    )(qT, k, vT, lam_vecs, subln_g_col)


def _lambda_init(layer_idx):
    return 0.8 - 0.6 * math.exp(-0.3 * layer_idx)


def _trunk(x, p):
    b, t, d = x.shape
    n = b * t
    rot = _rotary_tables(t)
    for i in range(2):
        if i == 0:
            qkv = _norm_qkv_na(x.reshape(n, d), p["attn_pre_g"][i], p["na_w_qkv"])
            a = _na_attention(qkv.reshape(b, t, 3 * d), p["na_bias"])
            w_o = p["na_w_o"]
        else:
            qT, k, vT = _norm_qkv_da(x, p["attn_pre_g"][i], p["da_wqT"], p["da_wk"], p["da_wvT"], rot)
            a = _da_attention(qT, k, vT, p["da_lam"], p["da_subln_g"], _lambda_init(i))
            w_o = p["da_w_o"]
        x = _proj_residual(x.reshape(n, d), a.reshape(n, d), w_o, p["attn_post_g"][i]).reshape(b, t, d)
        x = _ffn(x, p["ffn_pre_g"][i], p["ffn_w_in"][i], p["ffn_conv_w"][i], p["ffn_conv_b"][i],
                 p["ffn_w_out"][i], p["ffn_post_g"][i])
    return x


def _prepare(attn_pre_g, attn_post_g, ffn_pre_g, ffn_post_g, na_w_qkv, na_rpb, na_w_o,
             da_w_q, da_w_k, da_w_v, da_lambda_q1, da_lambda_k1, da_lambda_q2, da_lambda_k2,
             da_subln_g, da_w_o, ffn_w_in, ffn_conv_w, ffn_conv_b, ffn_w_out):
    depth = ffn_w_in.shape[0]
    nc = D_FF // FFN_CHUNK

    def chunk_cols(w):
        lead = w.shape[:-1]
        w = w.reshape(lead + (2 * nc, FFN_CHUNK))
        return jnp.moveaxis(w, -2, 0)

    row = lambda v: v.reshape(v.shape[0], 1, v.shape[1]).astype(F32)
    return {
        "attn_pre_g": row(attn_pre_g), "attn_post_g": row(attn_post_g),
        "ffn_pre_g": row(ffn_pre_g), "ffn_post_g": row(ffn_post_g),
        "na_w_qkv": na_w_qkv[0].astype(BF16),
        "na_bias": _na_bias_table(na_rpb[0]),
        "na_w_o": na_w_o[0].astype(BF16),
        "da_wqT": da_w_q[0].T.astype(BF16),
        "da_wk": da_w_k[0].astype(BF16),
        "da_wvT": da_w_v[0].T.astype(BF16),
        "da_lam": jnp.stack([da_lambda_q1[0], da_lambda_k1[0], da_lambda_q2[0], da_lambda_k2[0]]).astype(F32),
        "da_subln_g": da_subln_g[0].astype(F32).reshape(2 * DA_HEAD_DIM, 1),
        "da_w_o": da_w_o[0].astype(BF16),
        "ffn_w_in": [chunk_cols(ffn_w_in[i]).astype(BF16) for i in range(depth)],
        "ffn_conv_w": [chunk_cols(ffn_conv_w[i]).astype(F32) for i in range(depth)],
        "ffn_conv_b": [chunk_cols(ffn_conv_b[i][None]).astype(F32) for i in range(depth)],
        "ffn_w_out": [ffn_w_out[i].reshape(nc, FFN_CHUNK, D_MODEL).astype(BF16) for i in range(depth)],
    }


def kernel(x_prompt, x_sample, attn_pre_g, attn_post_g, ffn_pre_g, ffn_post_g, na_w_qkv, na_rpb, na_w_o,
           da_w_q, da_w_k, da_w_v, da_lambda_q1, da_lambda_k1, da_lambda_q2, da_lambda_k2, da_subln_g, da_w_o,
           ffn_w_in, ffn_conv_w, ffn_conv_b, ffn_w_out):
    p = _prepare(attn_pre_g, attn_post_g, ffn_pre_g, ffn_post_g, na_w_qkv, na_rpb, na_w_o,
                 da_w_q, da_w_k, da_w_v, da_lambda_q1, da_lambda_k1, da_lambda_q2, da_lambda_k2,
                 da_subln_g, da_w_o, ffn_w_in, ffn_conv_w, ffn_conv_b, ffn_w_out)
    return (_trunk(x_prompt, p), _trunk(x_sample, p))
```

```python
import functools
import math

import numpy as np
import jax
import jax.numpy as jnp
from jax import lax
from jax.experimental import pallas as pl
from jax.experimental.pallas import tpu as pltpu

D_MODEL = 1024
GRID_W = 64
NA_HEADS = 16
NA_HEAD_DIM = 64
NA_WIN_ROWS = 8
NA_WIN_COLS = 16
DA_HEADS = 8
DA_HEAD_DIM = 64
DA_SUBLN_EPS = 1e-5
ROPE_THETA = 10000.0
D_FF = 2816
NORM_EPS = 1e-6

F32 = jnp.float32
BF16 = jnp.bfloat16
LOG2E = 1.4426950408889634
MASK_VALUE = -1e30

VMEM_LIMIT_BYTES = 56 * 1024 * 1024

TOKEN_TILE = 512
NA_ROWS_PER_STEP = 8
NA_WINDOW_FETCH_ROWS = 16
NA_PIPELINE_DEPTH = 2
DA_Q_TILE = 512
DA_K_TILE = 512
DA_ONES_ROWS = 16
FFN_CHUNK = 256
CONV_HALO = 8


def _params(*semantics):
    return pltpu.CompilerParams(dimension_semantics=semantics, vmem_limit_bytes=VMEM_LIMIT_BYTES)


def _rms(x, g, eps=NORM_EPS):
    ms = jnp.mean(x * x, axis=-1, keepdims=True)
    return x * lax.rsqrt(ms + eps) * g


def _resident(shape):
    nd = len(shape)
    return pl.BlockSpec(shape, lambda *_: (0,) * nd, pipeline_mode=pl.Buffered(1))


def _norm_qkv_na_kernel(x_ref, g_ref, w_ref, o_ref, *, q_scale):
    xn = _rms(x_ref[...], g_ref[...]).astype(BF16)
    for c in range(3):
        r = jnp.dot(xn, w_ref[:, c * D_MODEL:(c + 1) * D_MODEL], preferred_element_type=F32)
        if c == 0:
            r = r * q_scale
        o_ref[:, c * D_MODEL:(c + 1) * D_MODEL] = r.astype(BF16)


def _norm_qkv_na(x2d, g, w_bf16):
    n = x2d.shape[0]
    tm = TOKEN_TILE
    return pl.pallas_call(
        functools.partial(_norm_qkv_na_kernel, q_scale=NA_HEAD_DIM ** -0.5),
        out_shape=jax.ShapeDtypeStruct((n, 3 * D_MODEL), BF16),
        grid=(n // tm,),
        in_specs=[
            pl.BlockSpec((tm, D_MODEL), lambda i: (i, 0)),
            _resident((1, D_MODEL)),
            _resident((D_MODEL, 3 * D_MODEL)),
        ],
        out_specs=pl.BlockSpec((tm, 3 * D_MODEL), lambda i: (i, 0)),
        compiler_params=_params("parallel"),
        name="norm_qkv_na",
    )(x2d, g, w_bf16)


def _na_window_start(i, rows):
    return jnp.clip(i * NA_ROWS_PER_STEP - NA_WIN_ROWS // 2, 0, rows - NA_WINDOW_FETCH_ROWS)


def _na_kernel(q_ref, k_ref, v_ref, bias_ref, o_ref, s_ref, *, rows):
    i = pl.program_id(1)
    w0 = _na_window_start(i, rows)
    lane = lax.broadcasted_iota(jnp.int32, (GRID_W, 2 * NA_HEAD_DIM), 1)
    first_head = lane < NA_HEAD_DIM
    n_keys = NA_WIN_ROWS * GRID_W

    def row_params(j):
        r = i * NA_ROWS_PER_STEP + j
        r0 = jnp.clip(r - NA_WIN_ROWS // 2, 0, rows - NA_WIN_ROWS)
        case = r0 - r + (NA_WIN_ROWS - 1)
        off = pl.multiple_of((r0 - w0) * GRID_W, GRID_W)
        return case, off

    def scores(j, p, slot, off):
        cols = slice(p * 2 * NA_HEAD_DIM, (p + 1) * 2 * NA_HEAD_DIM)
        qp = q_ref[j * GRID_W:(j + 1) * GRID_W, cols]
        zero = jnp.zeros_like(qp)
        qz = jnp.concatenate([jnp.where(first_head, qp, zero), jnp.where(first_head, zero, qp)], axis=0)
        kp = k_ref[0, pl.ds(off, n_keys), cols]
        s_ref[slot] = lax.dot_general(qz, kp, (((1,), (1,)), ((), ())), preferred_element_type=F32)

    def finish(j, p, slot, case, off):
        cols = slice(p * 2 * NA_HEAD_DIM, (p + 1) * 2 * NA_HEAD_DIM)
        es, ls = [], []
        for hh in range(2):
            rs = slice(hh * GRID_W, (hh + 1) * GRID_W)
            s = s_ref[slot, rs] + bias_ref[case, p, rs]
            m = jnp.max(s, axis=-1, keepdims=True)
            e = jnp.exp(s - m)
            ls.append(jnp.sum(e, axis=-1, keepdims=True))
            es.append(e.astype(BF16))
        vp = v_ref[0, pl.ds(off, n_keys), cols]
        pv = jnp.dot(jnp.concatenate(es, axis=0), vp, preferred_element_type=F32)
        o = jnp.where(first_head, pv[:GRID_W] / ls[0], pv[GRID_W:] / ls[1])
        o_ref[j * GRID_W:(j + 1) * GRID_W, cols] = o.astype(BF16)

    params = [row_params(j) for j in range(NA_ROWS_PER_STEP)]
    pending = []
    n_slots = NA_PIPELINE_DEPTH + 1
    for j in range(NA_ROWS_PER_STEP):
        case, off = params[j]
        for p in range(NA_HEADS // 2):
            slot = (j * (NA_HEADS // 2) + p) % n_slots
            scores(j, p, slot, off)
            pending.append((j, p, slot, case, off))
            if len(pending) > NA_PIPELINE_DEPTH:
                finish(*pending.pop(0))
    for item in pending:
        finish(*item)


def _na_attention(qkv, bias):
    b, t, _ = qkv.shape
    rows = t // GRID_W
    tq = NA_ROWS_PER_STEP * GRID_W
    tw = NA_WINDOW_FETCH_ROWS * GRID_W
    assert rows % NA_ROWS_PER_STEP == 0 and rows >= NA_WINDOW_FETCH_ROWS

    def kv_spec(col):
        return pl.BlockSpec((pl.Element(1), pl.Element(tw), pl.Element(D_MODEL)),
                            lambda bi, i: (bi, _na_window_start(i, rows) * GRID_W, col * D_MODEL))

    return pl.pallas_call(
        functools.partial(_na_kernel, rows=rows),
        out_shape=jax.ShapeDtypeStruct((b, t, D_MODEL), BF16),
        grid=(b, rows // NA_ROWS_PER_STEP),
        in_specs=[
            pl.BlockSpec((None, tq, D_MODEL), lambda bi, i: (bi, i, 0)),
            kv_spec(1),
            kv_spec(2),
            _resident(bias.shape),
        ],
        out_specs=pl.BlockSpec((None, tq, D_MODEL), lambda bi, i: (bi, i, 0)),
        scratch_shapes=[pltpu.VMEM((NA_PIPELINE_DEPTH + 1, 2 * GRID_W, NA_WIN_ROWS * GRID_W), F32)],
        compiler_params=_params("parallel", "parallel"),
        name="na_attention",
    )(qkv, qkv, qkv, bias)


def _na_bias_table(rpb):
    cols = np.arange(GRID_W)
    col_start = np.clip(cols - NA_WIN_COLS // 2, 0, GRID_W - NA_WIN_COLS)
    valid = (cols[None, :] >= col_start[:, None]) & (cols[None, :] < col_start[:, None] + NA_WIN_COLS)
    pad = GRID_W - NA_WIN_COLS
    rp = jnp.pad(rpb.astype(F32), ((0, 0), (0, 0), (pad, pad)))
    u = jnp.stack([rp[:, :, GRID_W - 1 - c:2 * GRID_W - 1 - c] for c in range(GRID_W)], axis=2)
    u = jnp.where(valid[None, None], u, MASK_VALUE)
    tab = jnp.stack([u[:, s:s + NA_WIN_ROWS] for s in range(NA_WIN_ROWS)], axis=0)
    tab = tab.transpose(0, 1, 3, 2, 4)
    return tab.reshape(NA_WIN_ROWS, NA_HEADS // 2, 2 * GRID_W, NA_WIN_ROWS * GRID_W)


def _proj_kernel(x_ref, a_ref, w_ref, g_ref, o_ref):
    m = jnp.dot(a_ref[...], w_ref[...], preferred_element_type=F32)
    o_ref[...] = x_ref[...] + _rms(m, g_ref[...])


def _proj_residual(x2d, a2d, w_bf16, g):
    n = x2d.shape[0]
    tm = TOKEN_TILE
    return pl.pallas_call(
        _proj_kernel,
        out_shape=jax.ShapeDtypeStruct((n, D_MODEL), F32),
        grid=(n // tm,),
        in_specs=[
            pl.BlockSpec((tm, D_MODEL), lambda i: (i, 0)),
            pl.BlockSpec((tm, D_MODEL), lambda i: (i, 0)),
            _resident((D_MODEL, D_MODEL)),
            _resident((1, D_MODEL)),
        ],
        out_specs=pl.BlockSpec((tm, D_MODEL), lambda i: (i, 0)),
        compiler_params=_params("parallel"),
        name="proj_residual",
    )(x2d, a2d, w_bf16, g)


def _gelu_tanh(x):
    return 0.5 * x * (1.0 + jnp.tanh(math.sqrt(2.0 / math.pi) * (x + 0.044715 * (x * x * x))))


def _ffn_kernel(x_ref, xprev_ref, xnext_ref, gpre_ref, win_ref, cw_ref, cb_ref, wout_ref, gpost_ref,
                o_ref, xcat_ref, xcb_ref, hg_a, hu_a, hg_b, hu_b, acc_ref, *, tm, n_chunks):
    i = pl.program_id(1)
    last = pl.num_programs(1) - 1
    h = CONV_HALO
    g = gpre_ref[...]
    x = x_ref[...]
    xcat_ref[0:h] = _rms(xprev_ref[...], g) * (i > 0).astype(F32)
    xcat_ref[h:tm + h] = _rms(x, g)
    xcat_ref[tm + h:tm + 2 * h] = _rms(xnext_ref[...], g) * (i < last).astype(F32)
    xcb_ref[...] = xcat_ref[...].astype(BF16)
    acc_ref[...] = jnp.zeros_like(acc_ref)

    def conv(h_ref, w, b):
        return (h_ref[h - 1:tm + h - 1] * w[0:1] + h_ref[h:tm + h] * w[1:2]
                + h_ref[h + 1:tm + h + 1] * w[2:3] + b)

    def up_proj(c, hg_ref, hu_ref):
        xc = xcb_ref[...]
        hg_ref[...] = jnp.dot(xc, win_ref[c], preferred_element_type=F32)
        hu_ref[...] = jnp.dot(xc, win_ref[n_chunks + c], preferred_element_type=F32)

    def gate_down(c, hg_ref, hu_ref):
        gate = conv(hg_ref, cw_ref[c], cb_ref[c])
        up = conv(hu_ref, cw_ref[n_chunks + c], cb_ref[n_chunks + c])
        act = (_gelu_tanh(gate) * up).astype(BF16)
        acc_ref[...] += jnp.dot(act, wout_ref[c], preferred_element_type=F32)

    up_proj(0, hg_a, hu_a)

    def pair_body(i, carry):
        c = 2 * i
        up_proj(c + 1, hg_b, hu_b)
        gate_down(c, hg_a, hu_a)
        up_proj(c + 2, hg_a, hu_a)
        gate_down(c + 1, hg_b, hu_b)
        return carry

    lax.fori_loop(0, (n_chunks - 1) // 2, pair_body, 0)
    gate_down(n_chunks - 1, hg_a, hu_a)
    o_ref[...] = x + _rms(acc_ref[...], gpost_ref[...])


def _ffn(x, gpre, win_c, cw_c, cb_c, wout_c, gpost):
    b, t, _ = x.shape
    tm = TOKEN_TILE
    h = CONV_HALO
    n_chunks = D_FF // FFN_CHUNK
    assert n_chunks % 2 == 1
    halo_blocks = tm // h
    n_halo = t // h
    return pl.pallas_call(
        functools.partial(_ffn_kernel, tm=tm, n_chunks=n_chunks),
        out_shape=jax.ShapeDtypeStruct((b, t, D_MODEL), F32),
        grid=(b, t // tm),
        in_specs=[
            pl.BlockSpec((None, tm, D_MODEL), lambda bi, i: (bi, i, 0)),
            pl.BlockSpec((None, h, D_MODEL), lambda bi, i: (bi, jnp.maximum(i * halo_blocks - 1, 0), 0)),
            pl.BlockSpec((None, h, D_MODEL), lambda bi, i: (bi, jnp.minimum((i + 1) * halo_blocks, n_halo - 1), 0)),
            _resident((1, D_MODEL)),
            _resident(win_c.shape),
            _resident(cw_c.shape),
            _resident(cb_c.shape),
            _resident(wout_c.shape),
            _resident((1, D_MODEL)),
        ],
        out_specs=pl.BlockSpec((None, tm, D_MODEL), lambda bi, i: (bi, i, 0)),
        scratch_shapes=[
            pltpu.VMEM((tm + 2 * h, D_MODEL), F32),
            pltpu.VMEM((tm + 2 * h, D_MODEL), BF16),
            pltpu.VMEM((tm + 2 * h, FFN_CHUNK), F32),
            pltpu.VMEM((tm + 2 * h, FFN_CHUNK), F32),
            pltpu.VMEM((tm + 2 * h, FFN_CHUNK), F32),
            pltpu.VMEM((tm + 2 * h, FFN_CHUNK), F32),
            pltpu.VMEM((tm, D_MODEL), F32),
        ],
        compiler_params=_params("parallel", "parallel"),
        name="conv_glu_ffn",
    )(x, x, x, gpre, win_c, cw_c, cb_c, wout_c, gpost)


def _norm_qkv_da_kernel(x_ref, g_ref, wqT_ref, wk_ref, wvT_ref, cos_ref, sin_ref, cosT_ref, sinT_ref,
                        qT_ref, k_ref, vT_ref, *, q_scale):
    xn = _rms(x_ref[...], g_ref[...]).astype(BF16)
    nt = (((1,), (1,)), ((), ()))
    half = DA_HEAD_DIM // 2

    kk = jnp.dot(xn, wk_ref[...], preferred_element_type=F32)
    cos = cos_ref[...]
    sin = sin_ref[...]
    lane = lax.broadcasted_iota(jnp.int32, cos.shape, 1)
    low = (lane % DA_HEAD_DIM) < half
    for c in range(D_MODEL // 128):
        xc = kk[:, c * 128:(c + 1) * 128]
        rot = jnp.where(low, pltpu.roll(xc, 128 - half, 1), pltpu.roll(xc, half, 1))
        k_ref[:, c * 128:(c + 1) * 128] = (xc * cos + rot * sin).astype(BF16)

    qT = lax.dot_general(wqT_ref[...], xn, nt, preferred_element_type=F32)
    cl, ch = cosT_ref[0:half], cosT_ref[half:DA_HEAD_DIM]
    sl, sh = sinT_ref[0:half], sinT_ref[half:DA_HEAD_DIM]
    for grp in range(D_MODEL // DA_HEAD_DIM):
        base = grp * DA_HEAD_DIM
        lo = qT[base:base + half]
        hi = qT[base + half:base + DA_HEAD_DIM]
        qT_ref[base:base + half] = ((lo * cl - hi * sl) * q_scale).astype(BF16)
        qT_ref[base + half:base + DA_HEAD_DIM] = ((hi * ch + lo * sh) * q_scale).astype(BF16)

    vT = lax.dot_general(wvT_ref[...], xn, nt, preferred_element_type=F32)
    for hd in range(DA_HEADS):
        vT_ref[hd] = vT[hd * 2 * DA_HEAD_DIM:(hd + 1) * 2 * DA_HEAD_DIM].astype(BF16)


def _norm_qkv_da(x, g, wqT, wk, wvT, tables):
    b, t, _ = x.shape
    tm = DA_K_TILE
    nk = t // tm
    cos128, sin128, cosT, sinT = tables
    hv = 2 * DA_HEAD_DIM
    return pl.pallas_call(
        functools.partial(_norm_qkv_da_kernel, q_scale=(DA_HEAD_DIM ** -0.5) * LOG2E),
        out_shape=(
            jax.ShapeDtypeStruct((b, D_MODEL, t), BF16),
            jax.ShapeDtypeStruct((b, t, D_MODEL), BF16),
            jax.ShapeDtypeStruct((b, DA_HEADS, nk, hv, tm), BF16),
        ),
        grid=(b, nk),
        in_specs=[
            pl.BlockSpec((None, tm, D_MODEL), lambda bi, i: (bi, i, 0)),
            _resident((1, D_MODEL)),
            _resident((D_MODEL, D_MODEL)),
            _resident((D_MODEL, D_MODEL)),
            _resident((D_MODEL, D_MODEL)),
            pl.BlockSpec((tm, 128), lambda bi, i: (i, 0)),
            pl.BlockSpec((tm, 128), lambda bi, i: (i, 0)),
            pl.BlockSpec((DA_HEAD_DIM, tm), lambda bi, i: (0, i)),
            pl.BlockSpec((DA_HEAD_DIM, tm), lambda bi, i: (0, i)),
        ],
        out_specs=(
            pl.BlockSpec((None, D_MODEL, tm), lambda bi, i: (bi, 0, i)),
            pl.BlockSpec((None, tm, D_MODEL), lambda bi, i: (bi, i, 0)),
            pl.BlockSpec((None, DA_HEADS, None, hv, tm), lambda bi, i: (bi, 0, i, 0, 0)),
        ),
        compiler_params=_params("parallel", "parallel"),
        name="norm_qkv_da",
    )(x, g, wqT, wk, wvT, cos128, sin128, cosT, sinT)


def _rotary_tables(t):
    half = DA_HEAD_DIM // 2
    inv = 1.0 / (ROPE_THETA ** (jnp.arange(0, DA_HEAD_DIM, 2, dtype=F32) / DA_HEAD_DIM))
    ang = jnp.arange(t, dtype=F32)[:, None] * inv[None, :]
    ang = jnp.concatenate([ang, ang], axis=-1)
    cos, sin = jnp.cos(ang), jnp.sin(ang)
    sign = jnp.where(jnp.arange(DA_HEAD_DIM) < half, -1.0, 1.0).astype(F32)
    cos128 = jnp.concatenate([cos, cos], axis=-1)
    sin128 = jnp.concatenate([sin * sign, sin * sign], axis=-1)
    return cos128, sin128, cos.T, sin.T


def _da_kernel(qT_ref, k_ref, vT_ref, lam_ref, g_ref, o_ref, qz_ref, st_a, st_b, mx_a, mx_b, m_ref, acc_ref,
               *, tq, tk, nk, lambda_init):
    hd = DA_HEAD_DIM
    hv = 2 * DA_HEAD_DIM
    qT = qT_ref[...]
    zero = jnp.zeros((hd, tq), BF16)
    qz_ref[0:hd, 0:tq] = qT[0:hd]
    qz_ref[hd:hv, 0:tq] = zero
    qz_ref[0:hd, tq:2 * tq] = zero
    qz_ref[hd:hv, tq:2 * tq] = qT[hd:hv]
    m_ref[...] = jnp.full(m_ref.shape, MASK_VALUE, F32)
    acc_ref[...] = jnp.zeros_like(acc_ref)
    ones = jnp.ones((DA_ONES_ROWS, tk), BF16)

    def scores(j, st_ref, mx_ref):
        kj = k_ref[pl.ds(pl.multiple_of(j * tk, tk), tk), :]
        st = jnp.dot(kj, qz_ref[...], preferred_element_type=F32)
        st_ref[...] = st
        mx_ref[...] = jnp.max(st, axis=0, keepdims=True)

    def accumulate(j, st_ref, mx_ref):
        m_old = m_ref[...]
        m_new = jnp.maximum(m_old, mx_ref[...])
        alpha = jnp.exp2(m_old - m_new)
        p = jnp.exp2(st_ref[...] - m_new).astype(BF16)
        vj = jnp.concatenate([vT_ref[j], ones], axis=0)
        acc_ref[...] = acc_ref[...] * alpha + jnp.dot(vj, p, preferred_element_type=F32)
        m_ref[...] = m_new

    scores(0, st_a, mx_a)

    def pair_body(i, carry):
        j = 2 * i
        scores(j + 1, st_b, mx_b)
        accumulate(j, st_a, mx_a)
        scores(j + 2, st_a, mx_a)
        accumulate(j + 1, st_b, mx_b)
        return carry

    lax.fori_loop(0, nk // 2 - 1, pair_body, 0)
    scores(nk - 1, st_b, mx_b)
    accumulate(nk - 2, st_a, mx_a)
    accumulate(nk - 1, st_b, mx_b)

    acc = acc_ref[...]
    o1 = acc[0:hv, 0:tq] / acc[hv:hv + 1, 0:tq]
    o2 = acc[0:hv, tq:2 * tq] / acc[hv:hv + 1, tq:2 * tq]
    lv = lam_ref[...]
    lam = (jnp.exp(jnp.sum(lv[0:1] * lv[1:2], axis=-1, keepdims=True))
           - jnp.exp(jnp.sum(lv[2:3] * lv[3:4], axis=-1, keepdims=True)) + lambda_init)
    o = o1 - lam * o2
    ms = jnp.mean(o * o, axis=0, keepdims=True)
    y = o * lax.rsqrt(ms + DA_SUBLN_EPS) * g_ref[...] * (1.0 - lambda_init)
    o_ref[...] = y.T.astype(BF16)


def _da_attention(qT, k, vT, lam_vecs, subln_g_col, lambda_init):
    b, t, _ = k.shape
    tq, tk = DA_Q_TILE, DA_K_TILE
    nk = t // tk
    hv = 2 * DA_HEAD_DIM
    assert nk >= 2 and nk % 2 == 0
    return pl.pallas_call(
        functools.partial(_da_kernel, tq=tq, tk=tk, nk=nk, lambda_init=lambda_init),
        out_shape=jax.ShapeDtypeStruct((b, t, D_MODEL), BF16),
        grid=(b, DA_HEADS, t // tq),
        in_specs=[
            pl.BlockSpec((None, hv, tq), lambda bi, h, qi: (bi, h, qi)),
            pl.BlockSpec((None, t, hv), lambda bi, h, qi: (bi, 0, h)),
            pl.BlockSpec((None, None, nk, hv, tk), lambda bi, h, qi: (bi, h, 0, 0, 0)),
            _resident(lam_vecs.shape),
            _resident(subln_g_col.shape),
        ],
        out_specs=pl.BlockSpec((None, tq, hv), lambda bi, h, qi: (bi, qi, h)),
        scratch_shapes=[
            pltpu.VMEM((hv, 2 * tq), BF16),
            pltpu.VMEM((tk, 2 * tq), F32),
            pltpu.VMEM((tk, 2 * tq), F32),
            pltpu.VMEM((1, 2 * tq), F32),
            pltpu.VMEM((1, 2 * tq), F32),
            pltpu.VMEM((1, 2 * tq), F32),
            pltpu.VMEM((hv + DA_ONES_ROWS, 2 * tq), F32),
        ],
        compiler_params=_params("parallel", "parallel", "arbitrary"),
        name="diff_attention",
    )(qT, k, vT, lam_vecs, subln_g_col)


def _lambda_init(layer_idx):
    return 0.8 - 0.6 * math.exp(-0.3 * layer_idx)


def _trunk(x, p):
    b, t, d = x.shape
    n = b * t
    rot = _rotary_tables(t)
    for i in range(2):
        if i == 0:
            qkv = _norm_qkv_na(x.reshape(n, d), p["attn_pre_g"][i], p["na_w_qkv"])
            a = _na_attention(qkv.reshape(b, t, 3 * d), p["na_bias"])
            w_o = p["na_w_o"]
        else:
            qT, k, vT = _norm_qkv_da(x, p["attn_pre_g"][i], p["da_wqT"], p["da_wk"], p["da_wvT"], rot)
            a = _da_attention(qT, k, vT, p["da_lam"], p["da_subln_g"], _lambda_init(i))
            w_o = p["da_w_o"]
        x = _proj_residual(x.reshape(n, d), a.reshape(n, d), w_o, p["attn_post_g"][i]).reshape(b, t, d)
        x = _ffn(x, p["ffn_pre_g"][i], p["ffn_w_in"][i], p["ffn_conv_w"][i], p["ffn_conv_b"][i],
                 p["ffn_w_out"][i], p["ffn_post_g"][i])
    return x


def _prepare(attn_pre_g, attn_post_g, ffn_pre_g, ffn_post_g, na_w_qkv, na_rpb, na_w_o,
             da_w_q, da_w_k, da_w_v, da_lambda_q1, da_lambda_k1, da_lambda_q2, da_lambda_k2,
             da_subln_g, da_w_o, ffn_w_in, ffn_conv_w, ffn_conv_b, ffn_w_out):
    depth = ffn_w_in.shape[0]
    nc = D_FF // FFN_CHUNK

    def chunk_cols(w):
        lead = w.shape[:-1]
        w = w.reshape(lead + (2 * nc, FFN_CHUNK))
        return jnp.moveaxis(w, -2, 0)

    row = lambda v: v.reshape(v.shape[0], 1, v.shape[1]).astype(F32)
    return {
        "attn_pre_g": row(attn_pre_g), "attn_post_g": row(attn_post_g),
        "ffn_pre_g": row(ffn_pre_g), "ffn_post_g": row(ffn_post_g),
        "na_w_qkv": na_w_qkv[0].astype(BF16),
        "na_bias": _na_bias_table(na_rpb[0]),
        "na_w_o": na_w_o[0].astype(BF16),
        "da_wqT": da_w_q[0].T.astype(BF16),
        "da_wk": da_w_k[0].astype(BF16),
        "da_wvT": da_w_v[0].T.astype(BF16),
        "da_lam": jnp.stack([da_lambda_q1[0], da_lambda_k1[0], da_lambda_q2[0], da_lambda_k2[0]]).astype(F32),
        "da_subln_g": da_subln_g[0].astype(F32).reshape(2 * DA_HEAD_DIM, 1),
        "da_w_o": da_w_o[0].astype(BF16),
        "ffn_w_in": [chunk_cols(ffn_w_in[i]).astype(BF16) for i in range(depth)],
        "ffn_conv_w": [chunk_cols(ffn_conv_w[i]).astype(F32) for i in range(depth)],
        "ffn_conv_b": [chunk_cols(ffn_conv_b[i][None]).astype(F32) for i in range(depth)],
        "ffn_w_out": [ffn_w_out[i].reshape(nc, FFN_CHUNK, D_MODEL).astype(BF16) for i in range(depth)],
    }


def kernel(x_prompt, x_sample, attn_pre_g, attn_post_g, ffn_pre_g, ffn_post_g, na_w_qkv, na_rpb, na_w_o,
           da_w_q, da_w_k, da_w_v, da_lambda_q1, da_lambda_k1, da_lambda_q2, da_lambda_k2, da_subln_g, da_w_o,
           ffn_w_in, ffn_conv_w, ffn_conv_b, ffn_w_out):
    p = _prepare(attn_pre_g, attn_post_g, ffn_pre_g, ffn_post_g, na_w_qkv, na_rpb, na_w_o,
                 da_w_q, da_w_k, da_w_v, da_lambda_q1, da_lambda_k1, da_lambda_q2, da_lambda_k2,
                 da_subln_g, da_w_o, ffn_w_in, ffn_conv_w, ffn_conv_b, ffn_w_out)
    return (_trunk(x_prompt, p), _trunk(x_sample, p))
```

```python
import functools
import math

import numpy as np
import jax
import jax.numpy as jnp
from jax import lax
from jax.experimental import pallas as pl
from jax.experimental.pallas import tpu as pltpu

D_MODEL = 1024
GRID_W = 64
NA_HEADS = 16
NA_HEAD_DIM = 64
NA_WIN_ROWS = 8
NA_WIN_COLS = 16
DA_HEADS = 8
DA_HEAD_DIM = 64
DA_SUBLN_EPS = 1e-5
ROPE_THETA = 10000.0
D_FF = 2816
NORM_EPS = 1e-6

F32 = jnp.float32
BF16 = jnp.bfloat16
LOG2E = 1.4426950408889634
MASK_VALUE = -1e30

VMEM_LIMIT_BYTES = 56 * 1024 * 1024

TOKEN_TILE = 512
NA_ROWS_PER_STEP = 8
NA_WINDOW_FETCH_ROWS = 16
NA_PIPELINE_DEPTH = 2
DA_Q_TILE = 512
DA_K_TILE = 512
DA_STAGE_PAIRS_PER_BODY = 4
DA_ONES_ROWS = 16
FFN_CHUNK = 256
CONV_HALO = 8


def _params(*semantics):
    return pltpu.CompilerParams(dimension_semantics=semantics, vmem_limit_bytes=VMEM_LIMIT_BYTES)


def _rms(x, g, eps=NORM_EPS):
    ms = jnp.mean(x * x, axis=-1, keepdims=True)
    return x * lax.rsqrt(ms + eps) * g


def _resident(shape):
    nd = len(shape)
    return pl.BlockSpec(shape, lambda *_: (0,) * nd, pipeline_mode=pl.Buffered(1))


def _norm_qkv_na_kernel(x_ref, g_ref, w_ref, o_ref, *, q_scale):
    xn = _rms(x_ref[...], g_ref[...]).astype(BF16)
    for c in range(3):
        r = jnp.dot(xn, w_ref[:, c * D_MODEL:(c + 1) * D_MODEL], preferred_element_type=F32)
        if c == 0:
            r = r * q_scale
        o_ref[:, c * D_MODEL:(c + 1) * D_MODEL] = r.astype(BF16)


def _norm_qkv_na(x2d, g, w_bf16):
    n = x2d.shape[0]
    tm = TOKEN_TILE
    return pl.pallas_call(
        functools.partial(_norm_qkv_na_kernel, q_scale=NA_HEAD_DIM ** -0.5),
        out_shape=jax.ShapeDtypeStruct((n, 3 * D_MODEL), BF16),
        grid=(n // tm,),
        in_specs=[
            pl.BlockSpec((tm, D_MODEL), lambda i: (i, 0)),
            _resident((1, D_MODEL)),
            _resident((D_MODEL, 3 * D_MODEL)),
        ],
        out_specs=pl.BlockSpec((tm, 3 * D_MODEL), lambda i: (i, 0)),
        compiler_params=_params("parallel"),
        name="norm_qkv_na",
    )(x2d, g, w_bf16)


def _na_window_start(i, rows):
    return jnp.clip(i * NA_ROWS_PER_STEP - NA_WIN_ROWS // 2, 0, rows - NA_WINDOW_FETCH_ROWS)


def _na_kernel(q_ref, k_ref, v_ref, bias_ref, o_ref, s_ref, *, rows):
    i = pl.program_id(1)
    w0 = _na_window_start(i, rows)
    lane = lax.broadcasted_iota(jnp.int32, (GRID_W, 2 * NA_HEAD_DIM), 1)
    first_head = lane < NA_HEAD_DIM
    n_keys = NA_WIN_ROWS * GRID_W

    def row_params(j):
        r = i * NA_ROWS_PER_STEP + j
        r0 = jnp.clip(r - NA_WIN_ROWS // 2, 0, rows - NA_WIN_ROWS)
        case = r0 - r + (NA_WIN_ROWS - 1)
        off = pl.multiple_of((r0 - w0) * GRID_W, GRID_W)
        return case, off

    def scores(j, p, slot, off):
        cols = slice(p * 2 * NA_HEAD_DIM, (p + 1) * 2 * NA_HEAD_DIM)
        qp = q_ref[j * GRID_W:(j + 1) * GRID_W, cols]
        zero = jnp.zeros_like(qp)
        qz = jnp.concatenate([jnp.where(first_head, qp, zero), jnp.where(first_head, zero, qp)], axis=0)
        kp = k_ref[0, pl.ds(off, n_keys), cols]
        s_ref[slot] = lax.dot_general(qz, kp, (((1,), (1,)), ((), ())), preferred_element_type=F32)

    def finish(j, p, slot, case, off):
        cols = slice(p * 2 * NA_HEAD_DIM, (p + 1) * 2 * NA_HEAD_DIM)
        es, ls = [], []
        for hh in range(2):
            rs = slice(hh * GRID_W, (hh + 1) * GRID_W)
            s = s_ref[slot, rs] + bias_ref[case, p, rs]
            m = jnp.max(s, axis=-1, keepdims=True)
            e = jnp.exp(s - m)
            ls.append(jnp.sum(e, axis=-1, keepdims=True))
            es.append(e.astype(BF16))
        vp = v_ref[0, pl.ds(off, n_keys), cols]
        pv = jnp.dot(jnp.concatenate(es, axis=0), vp, preferred_element_type=F32)
        o = jnp.where(first_head, pv[:GRID_W] / ls[0], pv[GRID_W:] / ls[1])
        o_ref[j * GRID_W:(j + 1) * GRID_W, cols] = o.astype(BF16)

    params = [row_params(j) for j in range(NA_ROWS_PER_STEP)]
    pending = []
    n_slots = NA_PIPELINE_DEPTH + 1
    for j in range(NA_ROWS_PER_STEP):
        case, off = params[j]
        for p in range(NA_HEADS // 2):
            slot = (j * (NA_HEADS // 2) + p) % n_slots
            scores(j, p, slot, off)
            pending.append((j, p, slot, case, off))
            if len(pending) > NA_PIPELINE_DEPTH:
                finish(*pending.pop(0))
    for item in pending:
        finish(*item)


def _na_attention(qkv, bias):
    b, t, _ = qkv.shape
    rows = t // GRID_W
    tq = NA_ROWS_PER_STEP * GRID_W
    tw = NA_WINDOW_FETCH_ROWS * GRID_W
    assert rows % NA_ROWS_PER_STEP == 0 and rows >= NA_WINDOW_FETCH_ROWS

    def kv_spec(col):
        return pl.BlockSpec((pl.Element(1), pl.Element(tw), pl.Element(D_MODEL)),
                            lambda bi, i: (bi, _na_window_start(i, rows) * GRID_W, col * D_MODEL))

    return pl.pallas_call(
        functools.partial(_na_kernel, rows=rows),
        out_shape=jax.ShapeDtypeStruct((b, t, D_MODEL), BF16),
        grid=(b, rows // NA_ROWS_PER_STEP),
        in_specs=[
            pl.BlockSpec((None, tq, D_MODEL), lambda bi, i: (bi, i, 0)),
            kv_spec(1),
            kv_spec(2),
            _resident(bias.shape),
        ],
        out_specs=pl.BlockSpec((None, tq, D_MODEL), lambda bi, i: (bi, i, 0)),
        scratch_shapes=[pltpu.VMEM((NA_PIPELINE_DEPTH + 1, 2 * GRID_W, NA_WIN_ROWS * GRID_W), F32)],
        compiler_params=_params("parallel", "parallel"),
        name="na_attention",
    )(qkv, qkv, qkv, bias)


def _na_bias_table(rpb):
    cols = np.arange(GRID_W)
    col_start = np.clip(cols - NA_WIN_COLS // 2, 0, GRID_W - NA_WIN_COLS)
    valid = (cols[None, :] >= col_start[:, None]) & (cols[None, :] < col_start[:, None] + NA_WIN_COLS)
    pad = GRID_W - NA_WIN_COLS
    rp = jnp.pad(rpb.astype(F32), ((0, 0), (0, 0), (pad, pad)))
    u = jnp.stack([rp[:, :, GRID_W - 1 - c:2 * GRID_W - 1 - c] for c in range(GRID_W)], axis=2)
    u = jnp.where(valid[None, None], u, MASK_VALUE)
    tab = jnp.stack([u[:, s:s + NA_WIN_ROWS] for s in range(NA_WIN_ROWS)], axis=0)
    tab = tab.transpose(0, 1, 3, 2, 4)
    return tab.reshape(NA_WIN_ROWS, NA_HEADS // 2, 2 * GRID_W, NA_WIN_ROWS * GRID_W)


def _proj_kernel(x_ref, a_ref, w_ref, g_ref, o_ref):
    if len(a_ref.shape) == 3:
        a = jnp.concatenate([a_ref[hd] for hd in range(a_ref.shape[0])], axis=1)
    else:
        a = a_ref[...]
    m = jnp.dot(a, w_ref[...], preferred_element_type=F32)
    o_ref[...] = x_ref[...] + _rms(m, g_ref[...])


def _proj_residual(x, a, w_bf16, g):
    b, t, _ = x.shape
    tm = TOKEN_TILE
    if a.ndim == 4:
        a_spec = pl.BlockSpec((None, a.shape[1], tm, a.shape[3]), lambda bi, i: (bi, 0, i, 0))
    else:
        a_spec = pl.BlockSpec((None, tm, D_MODEL), lambda bi, i: (bi, i, 0))
    return pl.pallas_call(
        _proj_kernel,
        out_shape=jax.ShapeDtypeStruct((b, t, D_MODEL), F32),
        grid=(b, t // tm),
        in_specs=[
            pl.BlockSpec((None, tm, D_MODEL), lambda bi, i: (bi, i, 0)),
            a_spec,
            _resident((D_MODEL, D_MODEL)),
            _resident((1, D_MODEL)),
        ],
        out_specs=pl.BlockSpec((None, tm, D_MODEL), lambda bi, i: (bi, i, 0)),
        compiler_params=_params("parallel", "parallel"),
        name="proj_residual",
    )(x, a, w_bf16, g)


def _gelu_tanh(x):
    return 0.5 * x * (1.0 + jnp.tanh(math.sqrt(2.0 / math.pi) * (x + 0.044715 * (x * x * x))))


def _ffn_kernel(x_ref, xprev_ref, xnext_ref, gpre_ref, win_ref, cw_ref, cb_ref, wout_ref, gpost_ref,
                o_ref, xcat_ref, xcb_ref, hg_a, hu_a, hg_b, hu_b, acc_ref, *, tm, n_chunks):
    i = pl.program_id(1)
    last = pl.num_programs(1) - 1
    h = CONV_HALO
    g = gpre_ref[...]
    x = x_ref[...]
    xcat_ref[0:h] = _rms(xprev_ref[...], g) * (i > 0).astype(F32)
    xcat_ref[h:tm + h] = _rms(x, g)
    xcat_ref[tm + h:tm + 2 * h] = _rms(xnext_ref[...], g) * (i < last).astype(F32)
    xcb_ref[...] = xcat_ref[...].astype(BF16)
    acc_ref[...] = jnp.zeros_like(acc_ref)

    def conv(h_ref, w, b):
        return (h_ref[h - 1:tm + h - 1] * w[0:1] + h_ref[h:tm + h] * w[1:2]
                + h_ref[h + 1:tm + h + 1] * w[2:3] + b)

    def up_proj(c, hg_ref, hu_ref):
        xc = xcb_ref[...]
        hg_ref[...] = jnp.dot(xc, win_ref[c], preferred_element_type=F32)
        hu_ref[...] = jnp.dot(xc, win_ref[n_chunks + c], preferred_element_type=F32)

    def gate_down(c, hg_ref, hu_ref):
        gate = conv(hg_ref, cw_ref[c], cb_ref[c])
        up = conv(hu_ref, cw_ref[n_chunks + c], cb_ref[n_chunks + c])
        act = (_gelu_tanh(gate) * up).astype(BF16)
        acc_ref[...] += jnp.dot(act, wout_ref[c], preferred_element_type=F32)

    up_proj(0, hg_a, hu_a)

    def pair_body(i, carry):
        c = 2 * i
        up_proj(c + 1, hg_b, hu_b)
        gate_down(c, hg_a, hu_a)
        up_proj(c + 2, hg_a, hu_a)
        gate_down(c + 1, hg_b, hu_b)
        return carry

    lax.fori_loop(0, (n_chunks - 1) // 2, pair_body, 0)
    gate_down(n_chunks - 1, hg_a, hu_a)
    o_ref[...] = x + _rms(acc_ref[...], gpost_ref[...])


def _ffn(x, gpre, win_c, cw_c, cb_c, wout_c, gpost):
    b, t, _ = x.shape
    tm = TOKEN_TILE
    h = CONV_HALO
    n_chunks = D_FF // FFN_CHUNK
    assert n_chunks % 2 == 1
    halo_blocks = tm // h
    n_halo = t // h
    return pl.pallas_call(
        functools.partial(_ffn_kernel, tm=tm, n_chunks=n_chunks),
        out_shape=jax.ShapeDtypeStruct((b, t, D_MODEL), F32),
        grid=(b, t // tm),
        in_specs=[
            pl.BlockSpec((None, tm, D_MODEL), lambda bi, i: (bi, i, 0)),
            pl.BlockSpec((None, h, D_MODEL), lambda bi, i: (bi, jnp.maximum(i * halo_blocks - 1, 0), 0)),
            pl.BlockSpec((None, h, D_MODEL), lambda bi, i: (bi, jnp.minimum((i + 1) * halo_blocks, n_halo - 1), 0)),
            _resident((1, D_MODEL)),
            _resident(win_c.shape),
            _resident(cw_c.shape),
            _resident(cb_c.shape),
            _resident(wout_c.shape),
            _resident((1, D_MODEL)),
        ],
        out_specs=pl.BlockSpec((None, tm, D_MODEL), lambda bi, i: (bi, i, 0)),
        scratch_shapes=[
            pltpu.VMEM((tm + 2 * h, D_MODEL), F32),
            pltpu.VMEM((tm + 2 * h, D_MODEL), BF16),
            pltpu.VMEM((tm + 2 * h, FFN_CHUNK), F32),
            pltpu.VMEM((tm + 2 * h, FFN_CHUNK), F32),
            pltpu.VMEM((tm + 2 * h, FFN_CHUNK), F32),
            pltpu.VMEM((tm + 2 * h, FFN_CHUNK), F32),
            pltpu.VMEM((tm, D_MODEL), F32),
        ],
        compiler_params=_params("parallel", "parallel"),
        name="conv_glu_ffn",
    )(x, x, x, gpre, win_c, cw_c, cb_c, wout_c, gpost)


def _norm_qkv_da_kernel(x_ref, g_ref, wqT_ref, wk_ref, wvT_ref, cos_ref, sin_ref, cosT_ref, sinT_ref,
                        qT_ref, k_ref, vT_ref, *, q_scale):
    xn = _rms(x_ref[...], g_ref[...]).astype(BF16)
    nt = (((1,), (1,)), ((), ()))
    half = DA_HEAD_DIM // 2

    kk = jnp.dot(xn, wk_ref[...], preferred_element_type=F32)
    cos = cos_ref[...]
    sin = sin_ref[...]
    lane = lax.broadcasted_iota(jnp.int32, cos.shape, 1)
    low = (lane % DA_HEAD_DIM) < half
    for c in range(D_MODEL // 128):
        xc = kk[:, c * 128:(c + 1) * 128]
        rot = jnp.where(low, pltpu.roll(xc, 128 - half, 1), pltpu.roll(xc, half, 1))
        k_ref[c] = (xc * cos + rot * sin).astype(BF16)

    qT = lax.dot_general(wqT_ref[...], xn, nt, preferred_element_type=F32)
    cl, ch = cosT_ref[0:half], cosT_ref[half:DA_HEAD_DIM]
    sl, sh = sinT_ref[0:half], sinT_ref[half:DA_HEAD_DIM]
    for grp in range(D_MODEL // DA_HEAD_DIM):
        base = grp * DA_HEAD_DIM
        lo = qT[base:base + half]
        hi = qT[base + half:base + DA_HEAD_DIM]
        qT_ref[base:base + half] = ((lo * cl - hi * sl) * q_scale).astype(BF16)
        qT_ref[base + half:base + DA_HEAD_DIM] = ((hi * ch + lo * sh) * q_scale).astype(BF16)

    vT = lax.dot_general(wvT_ref[...], xn, nt, preferred_element_type=F32)
    for hd in range(DA_HEADS):
        vT_ref[hd] = vT[hd * 2 * DA_HEAD_DIM:(hd + 1) * 2 * DA_HEAD_DIM].astype(BF16)


def _norm_qkv_da(x, g, wqT, wk, wvT, tables):
    b, t, _ = x.shape
    tm = DA_K_TILE
    nk = t // tm
    cos128, sin128, cosT, sinT = tables
    hv = 2 * DA_HEAD_DIM
    return pl.pallas_call(
        functools.partial(_norm_qkv_da_kernel, q_scale=(DA_HEAD_DIM ** -0.5) * LOG2E),
        out_shape=(
            jax.ShapeDtypeStruct((b, D_MODEL, t), BF16),
            jax.ShapeDtypeStruct((b, DA_HEADS, t, hv), BF16),
            jax.ShapeDtypeStruct((b, DA_HEADS, nk, hv, tm), BF16),
        ),
        grid=(b, nk),
        in_specs=[
            pl.BlockSpec((None, tm, D_MODEL), lambda bi, i: (bi, i, 0)),
            _resident((1, D_MODEL)),
            _resident((D_MODEL, D_MODEL)),
            _resident((D_MODEL, D_MODEL)),
            _resident((D_MODEL, D_MODEL)),
            pl.BlockSpec((tm, 128), lambda bi, i: (i, 0)),
            pl.BlockSpec((tm, 128), lambda bi, i: (i, 0)),
            pl.BlockSpec((DA_HEAD_DIM, tm), lambda bi, i: (0, i)),
            pl.BlockSpec((DA_HEAD_DIM, tm), lambda bi, i: (0, i)),
        ],
        out_specs=(
            pl.BlockSpec((None, D_MODEL, tm), lambda bi, i: (bi, 0, i)),
            pl.BlockSpec((None, DA_HEADS, tm, hv), lambda bi, i: (bi, 0, i, 0)),
            pl.BlockSpec((None, DA_HEADS, None, hv, tm), lambda bi, i: (bi, 0, i, 0, 0)),
        ),
        compiler_params=_params("parallel", "parallel"),
        name="norm_qkv_da",
    )(x, g, wqT, wk, wvT, cos128, sin128, cosT, sinT)


def _rotary_tables(t):
    half = DA_HEAD_DIM // 2
    inv = 1.0 / (ROPE_THETA ** (jnp.arange(0, DA_HEAD_DIM, 2, dtype=F32) / DA_HEAD_DIM))
    ang = jnp.arange(t, dtype=F32)[:, None] * inv[None, :]
    ang = jnp.concatenate([ang, ang], axis=-1)
    cos, sin = jnp.cos(ang), jnp.sin(ang)
    sign = jnp.where(jnp.arange(DA_HEAD_DIM) < half, -1.0, 1.0).astype(F32)
    cos128 = jnp.concatenate([cos, cos], axis=-1)
    sin128 = jnp.concatenate([sin * sign, sin * sign], axis=-1)
    return cos128, sin128, cos.T, sin.T


def _da_kernel(qT_ref, qT_next_ref, k_ref, vT_ref, lam_ref, g_ref, o_ref,
               qz_ref, qz_next_ref, st_a, st_b, mx_a, mx_b, m_ref, acc_ref, *, tq, tk, nk, lambda_init):
    hd = DA_HEAD_DIM
    hv = 2 * DA_HEAD_DIM
    zero = jnp.zeros((hd, tq), BF16)

    def build_qz(src_ref, dst_ref):
        qT = src_ref[...]
        dst_ref[0:hd, 0:tq] = qT[0:hd]
        dst_ref[hd:hv, 0:tq] = zero
        dst_ref[0:hd, tq:2 * tq] = zero
        dst_ref[hd:hv, tq:2 * tq] = qT[hd:hv]

    def scores(j, zq_ref, st_ref, mx_ref):
        kj = k_ref[pl.ds(pl.multiple_of(j * tk, tk), tk), :]
        st = jnp.dot(kj, zq_ref[...], preferred_element_type=F32)
        st_ref[...] = st
        mx_ref[...] = jnp.max(st, axis=0, keepdims=True)

    ones = jnp.ones((DA_ONES_ROWS, tk), BF16)

    def accumulate(j, st_ref, mx_ref):
        m_old = m_ref[...]
        m_new = jnp.maximum(m_old, mx_ref[...])
        alpha = jnp.exp2(m_old - m_new)
        p = jnp.exp2(st_ref[...] - m_new).astype(BF16)
        vj = jnp.concatenate([vT_ref[j], ones], axis=0)
        acc_ref[...] = acc_ref[...] * alpha + jnp.dot(vj, p, preferred_element_type=F32)
        m_ref[...] = m_new

    bufs = ((st_a, mx_a), (st_b, mx_b))
    build_qz(qT_ref, qz_ref)
    m_ref[...] = jnp.full(m_ref.shape, MASK_VALUE, F32)
    acc_ref[...] = jnp.zeros_like(acc_ref)

    @pl.when(pl.program_id(2) == 0)
    def _():
        scores(0, qz_ref, *bufs[0])

    def stage_pair(j, parity):
        scores(j + 1, qz_ref, *bufs[1 - parity])
        accumulate(j, *bufs[parity])

    unroll = DA_STAGE_PAIRS_PER_BODY
    n_body = (nk - 1) // unroll

    def body(i, carry):
        for u in range(unroll):
            stage_pair(i * unroll + u, u % 2)
        return carry

    lax.fori_loop(0, n_body, body, 0)
    for j in range(n_body * unroll, nk - 1):
        stage_pair(j, j % 2)
    build_qz(qT_next_ref, qz_next_ref)
    scores(0, qz_next_ref, *bufs[0])
    accumulate(nk - 1, *bufs[1])

    acc = acc_ref[...]
    o1 = acc[0:hv, 0:tq] / acc[hv:hv + 1, 0:tq]
    o2 = acc[0:hv, tq:2 * tq] / acc[hv:hv + 1, tq:2 * tq]
    lv = lam_ref[...]
    lam = (jnp.exp(jnp.sum(lv[0:1] * lv[1:2], axis=-1, keepdims=True))
           - jnp.exp(jnp.sum(lv[2:3] * lv[3:4], axis=-1, keepdims=True)) + lambda_init)
    o = o1 - lam * o2
    ms = jnp.mean(o * o, axis=0, keepdims=True)
    y = o * lax.rsqrt(ms + DA_SUBLN_EPS) * g_ref[...] * (1.0 - lambda_init)
    o_ref[...] = y.T.astype(BF16)


def _da_attention(qT, k, vT, lam_vecs, subln_g_col, lambda_init):
    b, _, t, _ = k.shape
    tq, tk = DA_Q_TILE, DA_K_TILE
    nk = t // tk
    hv = 2 * DA_HEAD_DIM
    nq = t // tq
    assert nk >= 2 and nk % 2 == 0
    return pl.pallas_call(
        functools.partial(_da_kernel, tq=tq, tk=tk, nk=nk, lambda_init=lambda_init),
        out_shape=jax.ShapeDtypeStruct((b, DA_HEADS, t, hv), BF16),
        grid=(b, DA_HEADS, nq),
        in_specs=[
            pl.BlockSpec((None, hv, tq), lambda bi, h, qi: (bi, h, qi)),
            pl.BlockSpec((None, hv, tq), lambda bi, h, qi: (bi, h, jnp.minimum(qi + 1, nq - 1))),
            pl.BlockSpec((None, None, t, hv), lambda bi, h, qi: (bi, h, 0, 0)),
            pl.BlockSpec((None, None, nk, hv, tk), lambda bi, h, qi: (bi, h, 0, 0, 0)),
            _resident(lam_vecs.shape),
            _resident(subln_g_col.shape),
        ],
        out_specs=pl.BlockSpec((None, None, tq, hv), lambda bi, h, qi: (bi, h, qi, 0)),
        scratch_shapes=[
            pltpu.VMEM((hv, 2 * tq), BF16),
            pltpu.VMEM((hv, 2 * tq), BF16),
            pltpu.VMEM((tk, 2 * tq), F32),
            pltpu.VMEM((tk, 2 * tq), F32),
            pltpu.VMEM((1, 2 * tq), F32),
            pltpu.VMEM((1, 2 * tq), F32),
            pltpu.VMEM((1, 2 * tq), F32),
            pltpu.VMEM((hv + DA_ONES_ROWS, 2 * tq), F32),
        ],
        compiler_params=_params("parallel", "parallel", "arbitrary"),
        name="diff_attention",
    )(qT, qT, k, vT, lam_vecs, subln_g_col)


def _lambda_init(layer_idx):
    return 0.8 - 0.6 * math.exp(-0.3 * layer_idx)


def _trunk(x, p):
    b, t, d = x.shape
    n = b * t
    rot = _rotary_tables(t)
    for i in range(2):
        if i == 0:
            qkv = _norm_qkv_na(x.reshape(n, d), p["attn_pre_g"][i], p["na_w_qkv"])
            a = _na_attention(qkv.reshape(b, t, 3 * d), p["na_bias"])
            w_o = p["na_w_o"]
        else:
            qT, k, vT = _norm_qkv_da(x, p["attn_pre_g"][i], p["da_wqT"], p["da_wk"], p["da_wvT"], rot)
            a = _da_attention(qT, k, vT, p["da_lam"], p["da_subln_g"], _lambda_init(i))
            w_o = p["da_w_o"]
        x = _proj_residual(x, a, w_o, p["attn_post_g"][i])
        x = _ffn(x, p["ffn_pre_g"][i], p["ffn_w_in"][i], p["ffn_conv_w"][i], p["ffn_conv_b"][i],
                 p["ffn_w_out"][i], p["ffn_post_g"][i])
    return x


def _prepare(attn_pre_g, attn_post_g, ffn_pre_g, ffn_post_g, na_w_qkv, na_rpb, na_w_o,
             da_w_q, da_w_k, da_w_v, da_lambda_q1, da_lambda_k1, da_lambda_q2, da_lambda_k2,
             da_subln_g, da_w_o, ffn_w_in, ffn_conv_w, ffn_conv_b, ffn_w_out):
    depth = ffn_w_in.shape[0]
    nc = D_FF // FFN_CHUNK

    def chunk_cols(w):
        lead = w.shape[:-1]
        w = w.reshape(lead + (2 * nc, FFN_CHUNK))
        return jnp.moveaxis(w, -2, 0)

    row = lambda v: v.reshape(v.shape[0], 1, v.shape[1]).astype(F32)
    return {
        "attn_pre_g": row(attn_pre_g), "attn_post_g": row(attn_post_g),
        "ffn_pre_g": row(ffn_pre_g), "ffn_post_g": row(ffn_post_g),
        "na_w_qkv": na_w_qkv[0].astype(BF16),
        "na_bias": _na_bias_table(na_rpb[0]),
        "na_w_o": na_w_o[0].astype(BF16),
        "da_wqT": da_w_q[0].T.astype(BF16),
        "da_wk": da_w_k[0].astype(BF16),
        "da_wvT": da_w_v[0].T.astype(BF16),
        "da_lam": jnp.stack([da_lambda_q1[0], da_lambda_k1[0], da_lambda_q2[0], da_lambda_k2[0]]).astype(F32),
        "da_subln_g": da_subln_g[0].astype(F32).reshape(2 * DA_HEAD_DIM, 1),
        "da_w_o": da_w_o[0].astype(BF16),
        "ffn_w_in": [chunk_cols(ffn_w_in[i]).astype(BF16) for i in range(depth)],
        "ffn_conv_w": [chunk_cols(ffn_conv_w[i]).astype(F32) for i in range(depth)],
        "ffn_conv_b": [chunk_cols(ffn_conv_b[i][None]).astype(F32) for i in range(depth)],
        "ffn_w_out": [ffn_w_out[i].reshape(nc, FFN_CHUNK, D_MODEL).astype(BF16) for i in range(depth)],
    }


def kernel(x_prompt, x_sample, attn_pre_g, attn_post_g, ffn_pre_g, ffn_post_g, na_w_qkv, na_rpb, na_w_o,
           da_w_q, da_w_k, da_w_v, da_lambda_q1, da_lambda_k1, da_lambda_q2, da_lambda_k2, da_subln_g, da_w_o,
           ffn_w_in, ffn_conv_w, ffn_conv_b, ffn_w_out):
    p = _prepare(attn_pre_g, attn_post_g, ffn_pre_g, ffn_post_g, na_w_qkv, na_rpb, na_w_o,
                 da_w_q, da_w_k, da_w_v, da_lambda_q1, da_lambda_k1, da_lambda_q2, da_lambda_k2,
                 da_subln_g, da_w_o, ffn_w_in, ffn_conv_w, ffn_conv_b, ffn_w_out)
    return (_trunk(x_prompt, p), _trunk(x_sample, p))
```

```python
import functools
import math

import numpy as np
import jax
import jax.numpy as jnp
from jax import lax
from jax.experimental import pallas as pl
from jax.experimental.pallas import tpu as pltpu

D_MODEL = 1024
GRID_W = 64
NA_HEADS = 16
NA_HEAD_DIM = 64
NA_WIN_ROWS = 8
NA_WIN_COLS = 16
DA_HEADS = 8
DA_HEAD_DIM = 64
DA_SUBLN_EPS = 1e-5
ROPE_THETA = 10000.0
D_FF = 2816
NORM_EPS = 1e-6

F32 = jnp.float32
BF16 = jnp.bfloat16
LOG2E = 1.4426950408889634
MASK_VALUE = -1e30

VMEM_LIMIT_BYTES = 56 * 1024 * 1024

TOKEN_TILE = 512
NA_ROWS_PER_STEP = 8
NA_WINDOW_FETCH_ROWS = 16
NA_PIPELINE_DEPTH = 2
DA_Q_TILE = 512
DA_K_TILE = 512
DA_STAGE_PAIRS_PER_BODY = 4
DA_ONES_ROWS = 16
FFN_CHUNK = 256
CONV_HALO = 8
ATTN_HALO = 16


def _params(*semantics):
    return pltpu.CompilerParams(dimension_semantics=semantics, vmem_limit_bytes=VMEM_LIMIT_BYTES)


def _rms(x, g, eps=NORM_EPS):
    ms = jnp.mean(x * x, axis=-1, keepdims=True)
    return x * lax.rsqrt(ms + eps) * g


def _resident(shape):
    nd = len(shape)
    return pl.BlockSpec(shape, lambda *_: (0,) * nd, pipeline_mode=pl.Buffered(1))


def _norm_qkv_na_kernel(x_ref, g_ref, w_ref, o_ref, *, q_scale):
    xn = _rms(x_ref[...], g_ref[...]).astype(BF16)
    for c in range(3):
        r = jnp.dot(xn, w_ref[:, c * D_MODEL:(c + 1) * D_MODEL], preferred_element_type=F32)
        if c == 0:
            r = r * q_scale
        o_ref[:, c * D_MODEL:(c + 1) * D_MODEL] = r.astype(BF16)


def _norm_qkv_na(x2d, g, w_bf16):
    n = x2d.shape[0]
    tm = TOKEN_TILE
    return pl.pallas_call(
        functools.partial(_norm_qkv_na_kernel, q_scale=NA_HEAD_DIM ** -0.5),
        out_shape=jax.ShapeDtypeStruct((n, 3 * D_MODEL), BF16),
        grid=(n // tm,),
        in_specs=[
            pl.BlockSpec((tm, D_MODEL), lambda i: (i, 0)),
            _resident((1, D_MODEL)),
            _resident((D_MODEL, 3 * D_MODEL)),
        ],
        out_specs=pl.BlockSpec((tm, 3 * D_MODEL), lambda i: (i, 0)),
        compiler_params=_params("parallel"),
        name="norm_qkv_na",
    )(x2d, g, w_bf16)


def _na_window_start(i, rows):
    return jnp.clip(i * NA_ROWS_PER_STEP - NA_WIN_ROWS // 2, 0, rows - NA_WINDOW_FETCH_ROWS)


def _na_kernel(q_ref, k_ref, v_ref, bias_ref, o_ref, s_ref, *, rows):
    i = pl.program_id(1)
    w0 = _na_window_start(i, rows)
    lane = lax.broadcasted_iota(jnp.int32, (GRID_W, 2 * NA_HEAD_DIM), 1)
    first_head = lane < NA_HEAD_DIM
    n_keys = NA_WIN_ROWS * GRID_W

    def row_params(j):
        r = i * NA_ROWS_PER_STEP + j
        r0 = jnp.clip(r - NA_WIN_ROWS // 2, 0, rows - NA_WIN_ROWS)
        case = r0 - r + (NA_WIN_ROWS - 1)
        off = pl.multiple_of((r0 - w0) * GRID_W, GRID_W)
        return case, off

    def scores(j, p, slot, off):
        cols = slice(p * 2 * NA_HEAD_DIM, (p + 1) * 2 * NA_HEAD_DIM)
        qp = q_ref[j * GRID_W:(j + 1) * GRID_W, cols]
        zero = jnp.zeros_like(qp)
        qz = jnp.concatenate([jnp.where(first_head, qp, zero), jnp.where(first_head, zero, qp)], axis=0)
        kp = k_ref[0, pl.ds(off, n_keys), cols]
        s_ref[slot] = lax.dot_general(qz, kp, (((1,), (1,)), ((), ())), preferred_element_type=F32)

    def finish(j, p, slot, case, off):
        cols = slice(p * 2 * NA_HEAD_DIM, (p + 1) * 2 * NA_HEAD_DIM)
        es, ls = [], []
        for hh in range(2):
            rs = slice(hh * GRID_W, (hh + 1) * GRID_W)
            s = s_ref[slot, rs] + bias_ref[case, p, rs]
            m = jnp.max(s, axis=-1, keepdims=True)
            e = jnp.exp(s - m)
            ls.append(jnp.sum(e, axis=-1, keepdims=True))
            es.append(e.astype(BF16))
        vp = v_ref[0, pl.ds(off, n_keys), cols]
        pv = jnp.dot(jnp.concatenate(es, axis=0), vp, preferred_element_type=F32)
        o = jnp.where(first_head, pv[:GRID_W] / ls[0], pv[GRID_W:] / ls[1])
        o_ref[j * GRID_W:(j + 1) * GRID_W, cols] = o.astype(BF16)

    params = [row_params(j) for j in range(NA_ROWS_PER_STEP)]
    pending = []
    n_slots = NA_PIPELINE_DEPTH + 1
    for j in range(NA_ROWS_PER_STEP):
        case, off = params[j]
        for p in range(NA_HEADS // 2):
            slot = (j * (NA_HEADS // 2) + p) % n_slots
            scores(j, p, slot, off)
            pending.append((j, p, slot, case, off))
            if len(pending) > NA_PIPELINE_DEPTH:
                finish(*pending.pop(0))
    for item in pending:
        finish(*item)


def _na_attention(qkv, bias):
    b, t, _ = qkv.shape
    rows = t // GRID_W
    tq = NA_ROWS_PER_STEP * GRID_W
    tw = NA_WINDOW_FETCH_ROWS * GRID_W
    assert rows % NA_ROWS_PER_STEP == 0 and rows >= NA_WINDOW_FETCH_ROWS

    def kv_spec(col):
        return pl.BlockSpec((pl.Element(1), pl.Element(tw), pl.Element(D_MODEL)),
                            lambda bi, i: (bi, _na_window_start(i, rows) * GRID_W, col * D_MODEL))

    return pl.pallas_call(
        functools.partial(_na_kernel, rows=rows),
        out_shape=jax.ShapeDtypeStruct((b, t, D_MODEL), BF16),
        grid=(b, rows // NA_ROWS_PER_STEP),
        in_specs=[
            pl.BlockSpec((None, tq, D_MODEL), lambda bi, i: (bi, i, 0)),
            kv_spec(1),
            kv_spec(2),
            _resident(bias.shape),
        ],
        out_specs=pl.BlockSpec((None, tq, D_MODEL), lambda bi, i: (bi, i, 0)),
        scratch_shapes=[pltpu.VMEM((NA_PIPELINE_DEPTH + 1, 2 * GRID_W, NA_WIN_ROWS * GRID_W), F32)],
        compiler_params=_params("parallel", "parallel"),
        name="na_attention",
    )(qkv, qkv, qkv, bias)


def _na_bias_table(rpb):
    cols = np.arange(GRID_W)
    col_start = np.clip(cols - NA_WIN_COLS // 2, 0, GRID_W - NA_WIN_COLS)
    valid = (cols[None, :] >= col_start[:, None]) & (cols[None, :] < col_start[:, None] + NA_WIN_COLS)
    pad = GRID_W - NA_WIN_COLS
    rp = jnp.pad(rpb.astype(F32), ((0, 0), (0, 0), (pad, pad)))
    u = jnp.stack([rp[:, :, GRID_W - 1 - c:2 * GRID_W - 1 - c] for c in range(GRID_W)], axis=2)
    u = jnp.where(valid[None, None], u, MASK_VALUE)
    tab = jnp.stack([u[:, s:s + NA_WIN_ROWS] for s in range(NA_WIN_ROWS)], axis=0)
    tab = tab.transpose(0, 1, 3, 2, 4)
    return tab.reshape(NA_WIN_ROWS, NA_HEADS // 2, 2 * GRID_W, NA_WIN_ROWS * GRID_W)


def _gelu_tanh(x):
    return 0.5 * x * (1.0 + jnp.tanh(math.sqrt(2.0 / math.pi) * (x + 0.044715 * (x * x * x))))


def _token_rows(ref):
    if len(ref.shape) == 3:
        return jnp.concatenate([ref[hd] for hd in range(ref.shape[0])], axis=1)
    return ref[...]


def _ffn_kernel(x_ref, xprev_ref, xnext_ref, a_ref, aprev_ref, anext_ref, wo_ref, gattn_ref,
                gpre_ref, win_ref, cw_ref, cb_ref, wout_ref, gpost_ref,
                o_ref, xcat_ref, xcb_ref, hg_a, hu_a, hg_b, hu_b, acc_ref, *, tm, n_chunks):
    i = pl.program_id(1)
    last = pl.num_programs(1) - 1
    h = CONV_HALO
    ah = ATTN_HALO
    a_cat = jnp.concatenate([_token_rows(aprev_ref), _token_rows(a_ref), _token_rows(anext_ref)], axis=0)
    x_cat = jnp.concatenate([xprev_ref[...], x_ref[...], xnext_ref[...]], axis=0)
    x_cat = x_cat + _rms(jnp.dot(a_cat, wo_ref[...], preferred_element_type=F32), gattn_ref[...])
    o_ref[...] = x_cat[ah:ah + tm]
    xn = _rms(x_cat, gpre_ref[...])
    xcat_ref[0:h] = xn[ah - h:ah] * (i > 0).astype(F32)
    xcat_ref[h:tm + h] = xn[ah:ah + tm]
    xcat_ref[tm + h:tm + 2 * h] = xn[ah + tm:ah + tm + h] * (i < last).astype(F32)
    xcb_ref[...] = xcat_ref[...].astype(BF16)
    acc_ref[...] = jnp.zeros_like(acc_ref)

    def conv(h_ref, w, b):
        return (h_ref[h - 1:tm + h - 1] * w[0:1] + h_ref[h:tm + h] * w[1:2]
                + h_ref[h + 1:tm + h + 1] * w[2:3] + b)

    def up_proj(c, hg_ref, hu_ref):
        xc = xcb_ref[...]
        hg_ref[...] = jnp.dot(xc, win_ref[c], preferred_element_type=F32)
        hu_ref[...] = jnp.dot(xc, win_ref[n_chunks + c], preferred_element_type=F32)

    def gate_down(c, hg_ref, hu_ref):
        gate = conv(hg_ref, cw_ref[c], cb_ref[c])
        up = conv(hu_ref, cw_ref[n_chunks + c], cb_ref[n_chunks + c])
        act = (_gelu_tanh(gate) * up).astype(BF16)
        acc_ref[...] += jnp.dot(act, wout_ref[c], preferred_element_type=F32)

    up_proj(0, hg_a, hu_a)

    def pair_body(i, carry):
        c = 2 * i
        up_proj(c + 1, hg_b, hu_b)
        gate_down(c, hg_a, hu_a)
        up_proj(c + 2, hg_a, hu_a)
        gate_down(c + 1, hg_b, hu_b)
        return carry

    lax.fori_loop(0, (n_chunks - 1) // 2, pair_body, 0)
    gate_down(n_chunks - 1, hg_a, hu_a)
    o_ref[...] += _rms(acc_ref[...], gpost_ref[...])


def _attn_proj_ffn(x, a, w_o, gattn, gpre, win_c, cw_c, cb_c, wout_c, gpost):
    b, t, _ = x.shape
    tm = TOKEN_TILE
    h = CONV_HALO
    ah = ATTN_HALO
    n_chunks = D_FF // FFN_CHUNK
    assert n_chunks % 2 == 1
    halo_blocks = tm // ah
    n_halo = t // ah
    prev_blk = lambda i: jnp.maximum(i * halo_blocks - 1, 0)
    next_blk = lambda i: jnp.minimum((i + 1) * halo_blocks, n_halo - 1)
    if a.ndim == 4:
        a_spec = lambda rows, blk: pl.BlockSpec((None, a.shape[1], rows, a.shape[3]),
                                                lambda bi, i: (bi, 0, blk(i), 0))
    else:
        a_spec = lambda rows, blk: pl.BlockSpec((None, rows, D_MODEL), lambda bi, i: (bi, blk(i), 0))
    return pl.pallas_call(
        functools.partial(_ffn_kernel, tm=tm, n_chunks=n_chunks),
        out_shape=jax.ShapeDtypeStruct((b, t, D_MODEL), F32),
        grid=(b, t // tm),
        in_specs=[
            pl.BlockSpec((None, tm, D_MODEL), lambda bi, i: (bi, i, 0)),
            pl.BlockSpec((None, ah, D_MODEL), lambda bi, i: (bi, prev_blk(i), 0)),
            pl.BlockSpec((None, ah, D_MODEL), lambda bi, i: (bi, next_blk(i), 0)),
            a_spec(tm, lambda i: i),
            a_spec(ah, prev_blk),
            a_spec(ah, next_blk),
            _resident((D_MODEL, D_MODEL)),
            _resident((1, D_MODEL)),
            _resident((1, D_MODEL)),
            _resident(win_c.shape),
            _resident(cw_c.shape),
            _resident(cb_c.shape),
            _resident(wout_c.shape),
            _resident((1, D_MODEL)),
        ],
        out_specs=pl.BlockSpec((None, tm, D_MODEL), lambda bi, i: (bi, i, 0)),
        scratch_shapes=[
            pltpu.VMEM((tm + 2 * h, D_MODEL), F32),
            pltpu.VMEM((tm + 2 * h, D_MODEL), BF16),
            pltpu.VMEM((tm + 2 * h, FFN_CHUNK), F32),
            pltpu.VMEM((tm + 2 * h, FFN_CHUNK), F32),
            pltpu.VMEM((tm + 2 * h, FFN_CHUNK), F32),
            pltpu.VMEM((tm + 2 * h, FFN_CHUNK), F32),
            pltpu.VMEM((tm, D_MODEL), F32),
        ],
        compiler_params=_params("parallel", "parallel"),
        name="attn_proj_conv_glu_ffn",
    )(x, x, x, a, a, a, w_o, gattn, gpre, win_c, cw_c, cb_c, wout_c, gpost)


def _norm_qkv_da_kernel(x_ref, g_ref, wqT_ref, wk_ref, wvT_ref, cos_ref, sin_ref, cosT_ref, sinT_ref,
                        qT_ref, k_ref, vT_ref, *, q_scale):
    xn = _rms(x_ref[...], g_ref[...]).astype(BF16)
    nt = (((1,), (1,)), ((), ()))
    half = DA_HEAD_DIM // 2

    kk = jnp.dot(xn, wk_ref[...], preferred_element_type=F32)
    cos = cos_ref[...]
    sin = sin_ref[...]
    lane = lax.broadcasted_iota(jnp.int32, cos.shape, 1)
    low = (lane % DA_HEAD_DIM) < half
    for c in range(D_MODEL // 128):
        xc = kk[:, c * 128:(c + 1) * 128]
        rot = jnp.where(low, pltpu.roll(xc, 128 - half, 1), pltpu.roll(xc, half, 1))
        k_ref[c] = (xc * cos + rot * sin).astype(BF16)

    qT = lax.dot_general(wqT_ref[...], xn, nt, preferred_element_type=F32)
    cl, ch = cosT_ref[0:half], cosT_ref[half:DA_HEAD_DIM]
    sl, sh = sinT_ref[0:half], sinT_ref[half:DA_HEAD_DIM]
    for grp in range(D_MODEL // DA_HEAD_DIM):
        base = grp * DA_HEAD_DIM
        lo = qT[base:base + half]
        hi = qT[base + half:base + DA_HEAD_DIM]
        qT_ref[base:base + half] = ((lo * cl - hi * sl) * q_scale).astype(BF16)
        qT_ref[base + half:base + DA_HEAD_DIM] = ((hi * ch + lo * sh) * q_scale).astype(BF16)

    vT = lax.dot_general(wvT_ref[...], xn, nt, preferred_element_type=F32)
    for hd in range(DA_HEADS):
        vT_ref[hd] = vT[hd * 2 * DA_HEAD_DIM:(hd + 1) * 2 * DA_HEAD_DIM].astype(BF16)


def _norm_qkv_da(x, g, wqT, wk, wvT, tables):
    b, t, _ = x.shape
    tm = DA_K_TILE
    nk = t // tm
    cos128, sin128, cosT, sinT = tables
    hv = 2 * DA_HEAD_DIM
    return pl.pallas_call(
        functools.partial(_norm_qkv_da_kernel, q_scale=(DA_HEAD_DIM ** -0.5) * LOG2E),
        out_shape=(
            jax.ShapeDtypeStruct((b, D_MODEL, t), BF16),
            jax.ShapeDtypeStruct((b, DA_HEADS, t, hv), BF16),
            jax.ShapeDtypeStruct((b, DA_HEADS, nk, hv, tm), BF16),
        ),
        grid=(b, nk),
        in_specs=[
            pl.BlockSpec((None, tm, D_MODEL), lambda bi, i: (bi, i, 0)),
            _resident((1, D_MODEL)),
            _resident((D_MODEL, D_MODEL)),
            _resident((D_MODEL, D_MODEL)),
            _resident((D_MODEL, D_MODEL)),
            pl.BlockSpec((tm, 128), lambda bi, i: (i, 0)),
            pl.BlockSpec((tm, 128), lambda bi, i: (i, 0)),
            pl.BlockSpec((DA_HEAD_DIM, tm), lambda bi, i: (0, i)),
            pl.BlockSpec((DA_HEAD_DIM, tm), lambda bi, i: (0, i)),
        ],
        out_specs=(
            pl.BlockSpec((None, D_MODEL, tm), lambda bi, i: (bi, 0, i)),
            pl.BlockSpec((None, DA_HEADS, tm, hv), lambda bi, i: (bi, 0, i, 0)),
            pl.BlockSpec((None, DA_HEADS, None, hv, tm), lambda bi, i: (bi, 0, i, 0, 0)),
        ),
        compiler_params=_params("parallel", "parallel"),
        name="norm_qkv_da",
    )(x, g, wqT, wk, wvT, cos128, sin128, cosT, sinT)


def _rotary_tables(t):
    half = DA_HEAD_DIM // 2
    inv = 1.0 / (ROPE_THETA ** (jnp.arange(0, DA_HEAD_DIM, 2, dtype=F32) / DA_HEAD_DIM))
    ang = jnp.arange(t, dtype=F32)[:, None] * inv[None, :]
    ang = jnp.concatenate([ang, ang], axis=-1)
    cos, sin = jnp.cos(ang), jnp.sin(ang)
    sign = jnp.where(jnp.arange(DA_HEAD_DIM) < half, -1.0, 1.0).astype(F32)
    cos128 = jnp.concatenate([cos, cos], axis=-1)
    sin128 = jnp.concatenate([sin * sign, sin * sign], axis=-1)
    return cos128, sin128, cos.T, sin.T


def _da_kernel(qT_ref, qT_next_ref, k_ref, vT_ref, lam_ref, g_ref, o_ref,
               qz_ref, qz_next_ref, st_a, st_b, mx_a, mx_b, m_ref, acc_ref, *, tq, tk, nk, lambda_init):
    hd = DA_HEAD_DIM
    hv = 2 * DA_HEAD_DIM
    zero = jnp.zeros((hd, tq), BF16)

    def build_qz(src_ref, dst_ref):
        qT = src_ref[...]
        dst_ref[0:hd, 0:tq] = qT[0:hd]
        dst_ref[hd:hv, 0:tq] = zero
        dst_ref[0:hd, tq:2 * tq] = zero
        dst_ref[hd:hv, tq:2 * tq] = qT[hd:hv]

    def scores(j, zq_ref, st_ref, mx_ref):
        kj = k_ref[pl.ds(pl.multiple_of(j * tk, tk), tk), :]
        st = jnp.dot(kj, zq_ref[...], preferred_element_type=F32)
        st_ref[...] = st
        mx_ref[...] = jnp.max(st, axis=0, keepdims=True)

    ones = jnp.ones((DA_ONES_ROWS, tk), BF16)

    def accumulate(j, st_ref, mx_ref):
        m_old = m_ref[...]
        m_new = jnp.maximum(m_old, mx_ref[...])
        alpha = jnp.exp2(m_old - m_new)
        p = jnp.exp2(st_ref[...] - m_new).astype(BF16)
        vj = jnp.concatenate([vT_ref[j], ones], axis=0)
        acc_ref[...] = acc_ref[...] * alpha + jnp.dot(vj, p, preferred_element_type=F32)
        m_ref[...] = m_new

    bufs = ((st_a, mx_a), (st_b, mx_b))
    build_qz(qT_ref, qz_ref)
    m_ref[...] = jnp.full(m_ref.shape, MASK_VALUE, F32)
    acc_ref[...] = jnp.zeros_like(acc_ref)

    @pl.when(pl.program_id(2) == 0)
    def _():
        scores(0, qz_ref, *bufs[0])

    def stage_pair(j, parity):
        scores(j + 1, qz_ref, *bufs[1 - parity])
        accumulate(j, *bufs[parity])

    unroll = DA_STAGE_PAIRS_PER_BODY
    n_body = (nk - 1) // unroll

    def body(i, carry):
        for u in range(unroll):
            stage_pair(i * unroll + u, u % 2)
        return carry

    lax.fori_loop(0, n_body, body, 0)
    for j in range(n_body * unroll, nk - 1):
        stage_pair(j, j % 2)
    build_qz(qT_next_ref, qz_next_ref)
    scores(0, qz_next_ref, *bufs[0])
    accumulate(nk - 1, *bufs[1])

    acc = acc_ref[...]
    o1 = acc[0:hv, 0:tq] / acc[hv:hv + 1, 0:tq]
    o2 = acc[0:hv, tq:2 * tq] / acc[hv:hv + 1, tq:2 * tq]
    lv = lam_ref[...]
    lam = (jnp.exp(jnp.sum(lv[0:1] * lv[1:2], axis=-1, keepdims=True))
           - jnp.exp(jnp.sum(lv[2:3] * lv[3:4], axis=-1, keepdims=True)) + lambda_init)
    o = o1 - lam * o2
    ms = jnp.mean(o * o, axis=0, keepdims=True)
    y = o * lax.rsqrt(ms + DA_SUBLN_EPS) * g_ref[...] * (1.0 - lambda_init)
    o_ref[...] = y.T.astype(BF16)


def _da_attention(qT, k, vT, lam_vecs, subln_g_col, lambda_init):
    b, _, t, _ = k.shape
    tq, tk = DA_Q_TILE, DA_K_TILE
    nk = t // tk
    hv = 2 * DA_HEAD_DIM
    nq = t // tq
    assert nk >= 2 and nk % 2 == 0
    return pl.pallas_call(
        functools.partial(_da_kernel, tq=tq, tk=tk, nk=nk, lambda_init=lambda_init),
        out_shape=jax.ShapeDtypeStruct((b, DA_HEADS, t, hv), BF16),
        grid=(b, DA_HEADS, nq),
        in_specs=[
            pl.BlockSpec((None, hv, tq), lambda bi, h, qi: (bi, h, qi)),
            pl.BlockSpec((None, hv, tq), lambda bi, h, qi: (bi, h, jnp.minimum(qi + 1, nq - 1))),
            pl.BlockSpec((None, None, t, hv), lambda bi, h, qi: (bi, h, 0, 0)),
            pl.BlockSpec((None, None, nk, hv, tk), lambda bi, h, qi: (bi, h, 0, 0, 0)),
            _resident(lam_vecs.shape),
            _resident(subln_g_col.shape),
        ],
        out_specs=pl.BlockSpec((None, None, tq, hv), lambda bi, h, qi: (bi, h, qi, 0)),
        scratch_shapes=[
            pltpu.VMEM((hv, 2 * tq), BF16),
            pltpu.VMEM((hv, 2 * tq), BF16),
            pltpu.VMEM((tk, 2 * tq), F32),
            pltpu.VMEM((tk, 2 * tq), F32),
            pltpu.VMEM((1, 2 * tq), F32),
            pltpu.VMEM((1, 2 * tq), F32),
            pltpu.VMEM((1, 2 * tq), F32),
            pltpu.VMEM((hv + DA_ONES_ROWS, 2 * tq), F32),
        ],
        compiler_params=_params("parallel", "parallel", "arbitrary"),
        name="diff_attention",
    )(qT, qT, k, vT, lam_vecs, subln_g_col)


def _lambda_init(layer_idx):
    return 0.8 - 0.6 * math.exp(-0.3 * layer_idx)


def _trunk(x, p):
    b, t, d = x.shape
    n = b * t
    rot = _rotary_tables(t)
    for i in range(2):
        if i == 0:
            qkv = _norm_qkv_na(x.reshape(n, d), p["attn_pre_g"][i], p["na_w_qkv"])
            a = _na_attention(qkv.reshape(b, t, 3 * d), p["na_bias"])
            w_o = p["na_w_o"]
        else:
            qT, k, vT = _norm_qkv_da(x, p["attn_pre_g"][i], p["da_wqT"], p["da_wk"], p["da_wvT"], rot)
            a = _da_attention(qT, k, vT, p["da_lam"], p["da_subln_g"], _lambda_init(i))
            w_o = p["da_w_o"]
        x = _attn_proj_ffn(x, a, w_o, p["attn_post_g"][i], p["ffn_pre_g"][i], p["ffn_w_in"][i],
                           p["ffn_conv_w"][i], p["ffn_conv_b"][i], p["ffn_w_out"][i], p["ffn_post_g"][i])
    return x


def _prepare(attn_pre_g, attn_post_g, ffn_pre_g, ffn_post_g, na_w_qkv, na_rpb, na_w_o,
             da_w_q, da_w_k, da_w_v, da_lambda_q1, da_lambda_k1, da_lambda_q2, da_lambda_k2,
             da_subln_g, da_w_o, ffn_w_in, ffn_conv_w, ffn_conv_b, ffn_w_out):
    depth = ffn_w_in.shape[0]
    nc = D_FF // FFN_CHUNK

    def chunk_cols(w):
        lead = w.shape[:-1]
        w = w.reshape(lead + (2 * nc, FFN_CHUNK))
        return jnp.moveaxis(w, -2, 0)

    row = lambda v: v.reshape(v.shape[0], 1, v.shape[1]).astype(F32)
    return {
        "attn_pre_g": row(attn_pre_g), "attn_post_g": row(attn_post_g),
        "ffn_pre_g": row(ffn_pre_g), "ffn_post_g": row(ffn_post_g),
        "na_w_qkv": na_w_qkv[0].astype(BF16),
        "na_bias": _na_bias_table(na_rpb[0]),
        "na_w_o": na_w_o[0].astype(BF16),
        "da_wqT": da_w_q[0].T.astype(BF16),
        "da_wk": da_w_k[0].astype(BF16),
        "da_wvT": da_w_v[0].T.astype(BF16),
        "da_lam": jnp.stack([da_lambda_q1[0], da_lambda_k1[0], da_lambda_q2[0], da_lambda_k2[0]]).astype(F32),
        "da_subln_g": da_subln_g[0].astype(F32).reshape(2 * DA_HEAD_DIM, 1),
        "da_w_o": da_w_o[0].astype(BF16),
        "ffn_w_in": [chunk_cols(ffn_w_in[i]).astype(BF16) for i in range(depth)],
        "ffn_conv_w": [chunk_cols(ffn_conv_w[i]).astype(F32) for i in range(depth)],
        "ffn_conv_b": [chunk_cols(ffn_conv_b[i][None]).astype(F32) for i in range(depth)],
        "ffn_w_out": [ffn_w_out[i].reshape(nc, FFN_CHUNK, D_MODEL).astype(BF16) for i in range(depth)],
    }


def kernel(x_prompt, x_sample, attn_pre_g, attn_post_g, ffn_pre_g, ffn_post_g, na_w_qkv, na_rpb, na_w_o,
           da_w_q, da_w_k, da_w_v, da_lambda_q1, da_lambda_k1, da_lambda_q2, da_lambda_k2, da_subln_g, da_w_o,
           ffn_w_in, ffn_conv_w, ffn_conv_b, ffn_w_out):
    p = _prepare(attn_pre_g, attn_post_g, ffn_pre_g, ffn_post_g, na_w_qkv, na_rpb, na_w_o,
                 da_w_q, da_w_k, da_w_v, da_lambda_q1, da_lambda_k1, da_lambda_q2, da_lambda_k2,
                 da_subln_g, da_w_o, ffn_w_in, ffn_conv_w, ffn_conv_b, ffn_w_out)
    return (_trunk(x_prompt, p), _trunk(x_sample, p))
```

```python
import functools
import math

import numpy as np
import jax
import jax.numpy as jnp
from jax import lax
from jax.experimental import pallas as pl
from jax.experimental.pallas import tpu as pltpu

D_MODEL = 1024
GRID_W = 64
NA_HEADS = 16
NA_HEAD_DIM = 64
NA_WIN_ROWS = 8
NA_WIN_COLS = 16
DA_HEADS = 8
DA_HEAD_DIM = 64
DA_SUBLN_EPS = 1e-5
ROPE_THETA = 10000.0
D_FF = 2816
NORM_EPS = 1e-6

F32 = jnp.float32
BF16 = jnp.bfloat16
LOG2E = 1.4426950408889634
MASK_VALUE = -1e30

VMEM_LIMIT_BYTES = 56 * 1024 * 1024

TOKEN_TILE = 512
NA_ROWS_PER_STEP = 8
NA_WINDOW_FETCH_ROWS = 16
NA_PIPELINE_DEPTH = 2
DA_Q_TILE = 512
DA_K_TILE = 512
DA_STAGE_PAIRS_PER_BODY = 4
DA_ONES_ROWS = 16
FFN_CHUNK = 256
CONV_HALO = 8
ATTN_HALO = 16


def _params(*semantics):
    return pltpu.CompilerParams(dimension_semantics=semantics, vmem_limit_bytes=VMEM_LIMIT_BYTES)


def _rms(x, g, eps=NORM_EPS):
    ms = jnp.mean(x * x, axis=-1, keepdims=True)
    return x * lax.rsqrt(ms + eps) * g


def _resident(shape):
    nd = len(shape)
    return pl.BlockSpec(shape, lambda *_: (0,) * nd, pipeline_mode=pl.Buffered(1))


def _norm_qkv_na_kernel(x_ref, g_ref, w_ref, o_ref, *, q_scale):
    xn = _rms(x_ref[...], g_ref[...]).astype(BF16)
    for c in range(3):
        r = jnp.dot(xn, w_ref[:, c * D_MODEL:(c + 1) * D_MODEL], preferred_element_type=F32)
        if c == 0:
            r = r * q_scale
        o_ref[:, c * D_MODEL:(c + 1) * D_MODEL] = r.astype(BF16)


def _norm_qkv_na(x2d, g, w_bf16):
    n = x2d.shape[0]
    tm = TOKEN_TILE
    return pl.pallas_call(
        functools.partial(_norm_qkv_na_kernel, q_scale=(NA_HEAD_DIM ** -0.5) * LOG2E),
        out_shape=jax.ShapeDtypeStruct((n, 3 * D_MODEL), BF16),
        grid=(n // tm,),
        in_specs=[
            pl.BlockSpec((tm, D_MODEL), lambda i: (i, 0)),
            _resident((1, D_MODEL)),
            _resident((D_MODEL, 3 * D_MODEL)),
        ],
        out_specs=pl.BlockSpec((tm, 3 * D_MODEL), lambda i: (i, 0)),
        compiler_params=_params("parallel"),
        name="norm_qkv_na",
    )(x2d, g, w_bf16)


def _na_window_start(i, rows):
    return jnp.clip(i * NA_ROWS_PER_STEP - NA_WIN_ROWS // 2, 0, rows - NA_WINDOW_FETCH_ROWS)


def _na_kernel(q_ref, k_ref, v_ref, bias_ref, o_ref, s_ref, *, rows):
    i = pl.program_id(1)
    w0 = _na_window_start(i, rows)
    lane = lax.broadcasted_iota(jnp.int32, (GRID_W, 2 * NA_HEAD_DIM), 1)
    first_head = lane < NA_HEAD_DIM
    n_keys = NA_WIN_ROWS * GRID_W
    ones = jnp.ones((n_keys, 2 * NA_HEAD_DIM), BF16)

    def row_params(j):
        r = i * NA_ROWS_PER_STEP + j
        r0 = jnp.clip(r - NA_WIN_ROWS // 2, 0, rows - NA_WIN_ROWS)
        case = r0 - r + (NA_WIN_ROWS - 1)
        off = pl.multiple_of((r0 - w0) * GRID_W, GRID_W)
        return case, off

    def scores(j, p, slot, off):
        cols = slice(p * 2 * NA_HEAD_DIM, (p + 1) * 2 * NA_HEAD_DIM)
        qp = q_ref[j * GRID_W:(j + 1) * GRID_W, cols]
        zero = jnp.zeros_like(qp)
        qz = jnp.concatenate([jnp.where(first_head, qp, zero), jnp.where(first_head, zero, qp)], axis=0)
        kp = k_ref[0, pl.ds(off, n_keys), cols]
        s_ref[slot] = lax.dot_general(qz, kp, (((1,), (1,)), ((), ())), preferred_element_type=F32)

    def finish(j, p, slot, case, off):
        cols = slice(p * 2 * NA_HEAD_DIM, (p + 1) * 2 * NA_HEAD_DIM)
        s = s_ref[slot] + bias_ref[case, p]
        e = jnp.exp2(s - jnp.max(s, axis=-1, keepdims=True)).astype(BF16)
        vp = jnp.concatenate([v_ref[0, pl.ds(off, n_keys), cols], ones], axis=1)
        pv = jnp.dot(e, vp, preferred_element_type=F32)
        width = 2 * NA_HEAD_DIM
        o = pv[:, :width] / pv[:, width:]
        o_ref[j * GRID_W:(j + 1) * GRID_W, cols] = jnp.where(first_head, o[:GRID_W], o[GRID_W:]).astype(BF16)

    params = [row_params(j) for j in range(NA_ROWS_PER_STEP)]
    pending = []
    n_slots = NA_PIPELINE_DEPTH + 1
    for j in range(NA_ROWS_PER_STEP):
        case, off = params[j]
        for p in range(NA_HEADS // 2):
            slot = (j * (NA_HEADS // 2) + p) % n_slots
            scores(j, p, slot, off)
            pending.append((j, p, slot, case, off))
            if len(pending) > NA_PIPELINE_DEPTH:
                finish(*pending.pop(0))
    for item in pending:
        finish(*item)


def _na_attention(qkv, bias):
    b, t, _ = qkv.shape
    rows = t // GRID_W
    tq = NA_ROWS_PER_STEP * GRID_W
    tw = NA_WINDOW_FETCH_ROWS * GRID_W
    assert rows % NA_ROWS_PER_STEP == 0 and rows >= NA_WINDOW_FETCH_ROWS

    def kv_spec(col):
        return pl.BlockSpec((pl.Element(1), pl.Element(tw), pl.Element(D_MODEL)),
                            lambda bi, i: (bi, _na_window_start(i, rows) * GRID_W, col * D_MODEL))

    return pl.pallas_call(
        functools.partial(_na_kernel, rows=rows),
        out_shape=jax.ShapeDtypeStruct((b, t, D_MODEL), BF16),
        grid=(b, rows // NA_ROWS_PER_STEP),
        in_specs=[
            pl.BlockSpec((None, tq, D_MODEL), lambda bi, i: (bi, i, 0)),
            kv_spec(1),
            kv_spec(2),
            _resident(bias.shape),
        ],
        out_specs=pl.BlockSpec((None, tq, D_MODEL), lambda bi, i: (bi, i, 0)),
        scratch_shapes=[pltpu.VMEM((NA_PIPELINE_DEPTH + 1, 2 * GRID_W, NA_WIN_ROWS * GRID_W), F32)],
        compiler_params=_params("parallel", "parallel"),
        name="na_attention",
    )(qkv, qkv, qkv, bias)


def _na_bias_table(rpb):
    cols = np.arange(GRID_W)
    col_start = np.clip(cols - NA_WIN_COLS // 2, 0, GRID_W - NA_WIN_COLS)
    valid = (cols[None, :] >= col_start[:, None]) & (cols[None, :] < col_start[:, None] + NA_WIN_COLS)
    pad = GRID_W - NA_WIN_COLS
    rp = jnp.pad(rpb.astype(F32), ((0, 0), (0, 0), (pad, pad)))
    u = jnp.stack([rp[:, :, GRID_W - 1 - c:2 * GRID_W - 1 - c] for c in range(GRID_W)], axis=2)
    u = jnp.where(valid[None, None], u * LOG2E, MASK_VALUE)
    tab = jnp.stack([u[:, s:s + NA_WIN_ROWS] for s in range(NA_WIN_ROWS)], axis=0)
    tab = tab.transpose(0, 1, 3, 2, 4)
    return tab.reshape(NA_WIN_ROWS, NA_HEADS // 2, 2 * GRID_W, NA_WIN_ROWS * GRID_W)


def _gelu_tanh(x):
    return 0.5 * x * (1.0 + jnp.tanh(math.sqrt(2.0 / math.pi) * (x + 0.044715 * (x * x * x))))


def _token_rows(ref):
    if len(ref.shape) == 3:
        return jnp.concatenate([ref[hd] for hd in range(ref.shape[0])], axis=1)
    return ref[...]


def _ffn_kernel(x_ref, xprev_ref, xnext_ref, a_ref, aprev_ref, anext_ref, wo_ref, gattn_ref,
                gpre_ref, win_ref, cw_ref, cb_ref, wout_ref, gpost_ref,
                o_ref, xcat_ref, xcb_ref, hg_a, hu_a, hg_b, hu_b, acc_ref, *, tm, n_chunks):
    i = pl.program_id(1)
    last = pl.num_programs(1) - 1
    h = CONV_HALO
    ah = ATTN_HALO
    a_cat = jnp.concatenate([_token_rows(aprev_ref), _token_rows(a_ref), _token_rows(anext_ref)], axis=0)
    x_cat = jnp.concatenate([xprev_ref[...], x_ref[...], xnext_ref[...]], axis=0)
    x_cat = x_cat + _rms(jnp.dot(a_cat, wo_ref[...], preferred_element_type=F32), gattn_ref[...])
    o_ref[...] = x_cat[ah:ah + tm]
    xn = _rms(x_cat, gpre_ref[...])
    xcat_ref[0:h] = xn[ah - h:ah] * (i > 0).astype(F32)
    xcat_ref[h:tm + h] = xn[ah:ah + tm]
    xcat_ref[tm + h:tm + 2 * h] = xn[ah + tm:ah + tm + h] * (i < last).astype(F32)
    xcb_ref[...] = xcat_ref[...].astype(BF16)
    acc_ref[...] = jnp.zeros_like(acc_ref)

    def conv(h_ref, w, b):
        return (h_ref[h - 1:tm + h - 1] * w[0:1] + h_ref[h:tm + h] * w[1:2]
                + h_ref[h + 1:tm + h + 1] * w[2:3] + b)

    def up_proj(c, hg_ref, hu_ref):
        xc = xcb_ref[...]
        hg_ref[...] = jnp.dot(xc, win_ref[c], preferred_element_type=F32)
        hu_ref[...] = jnp.dot(xc, win_ref[n_chunks + c], preferred_element_type=F32)

    def gate_down(c, hg_ref, hu_ref):
        gate = conv(hg_ref, cw_ref[c], cb_ref[c])
        up = conv(hu_ref, cw_ref[n_chunks + c], cb_ref[n_chunks + c])
        act = (_gelu_tanh(gate) * up).astype(BF16)
        acc_ref[...] += jnp.dot(act, wout_ref[c], preferred_element_type=F32)

    up_proj(0, hg_a, hu_a)

    def pair_body(i, carry):
        c = 2 * i
        up_proj(c + 1, hg_b, hu_b)
        gate_down(c, hg_a, hu_a)
        up_proj(c + 2, hg_a, hu_a)
        gate_down(c + 1, hg_b, hu_b)
        return carry

    lax.fori_loop(0, (n_chunks - 1) // 2, pair_body, 0)
    gate_down(n_chunks - 1, hg_a, hu_a)
    o_ref[...] += _rms(acc_ref[...], gpost_ref[...])


def _attn_proj_ffn(x, a, w_o, gattn, gpre, win_c, cw_c, cb_c, wout_c, gpost):
    b, t, _ = x.shape
    tm = TOKEN_TILE
    h = CONV_HALO
    ah = ATTN_HALO
    n_chunks = D_FF // FFN_CHUNK
    assert n_chunks % 2 == 1
    halo_blocks = tm // ah
    n_halo = t // ah
    prev_blk = lambda i: jnp.maximum(i * halo_blocks - 1, 0)
    next_blk = lambda i: jnp.minimum((i + 1) * halo_blocks, n_halo - 1)
    if a.ndim == 4:
        a_spec = lambda rows, blk: pl.BlockSpec((None, a.shape[1], rows, a.shape[3]),
                                                lambda bi, i: (bi, 0, blk(i), 0))
    else:
        a_spec = lambda rows, blk: pl.BlockSpec((None, rows, D_MODEL), lambda bi, i: (bi, blk(i), 0))
    return pl.pallas_call(
        functools.partial(_ffn_kernel, tm=tm, n_chunks=n_chunks),
        out_shape=jax.ShapeDtypeStruct((b, t, D_MODEL), F32),
        grid=(b, t // tm),
        in_specs=[
            pl.BlockSpec((None, tm, D_MODEL), lambda bi, i: (bi, i, 0)),
            pl.BlockSpec((None, ah, D_MODEL), lambda bi, i: (bi, prev_blk(i), 0)),
            pl.BlockSpec((None, ah, D_MODEL), lambda bi, i: (bi, next_blk(i), 0)),
            a_spec(tm, lambda i: i),
            a_spec(ah, prev_blk),
            a_spec(ah, next_blk),
            _resident((D_MODEL, D_MODEL)),
            _resident((1, D_MODEL)),
            _resident((1, D_MODEL)),
            _resident(win_c.shape),
            _resident(cw_c.shape),
            _resident(cb_c.shape),
            _resident(wout_c.shape),
            _resident((1, D_MODEL)),
        ],
        out_specs=pl.BlockSpec((None, tm, D_MODEL), lambda bi, i: (bi, i, 0)),
        scratch_shapes=[
            pltpu.VMEM((tm + 2 * h, D_MODEL), F32),
            pltpu.VMEM((tm + 2 * h, D_MODEL), BF16),
            pltpu.VMEM((tm + 2 * h, FFN_CHUNK), F32),
            pltpu.VMEM((tm + 2 * h, FFN_CHUNK), F32),
            pltpu.VMEM((tm + 2 * h, FFN_CHUNK), F32),
            pltpu.VMEM((tm + 2 * h, FFN_CHUNK), F32),
            pltpu.VMEM((tm, D_MODEL), F32),
        ],
        compiler_params=_params("parallel", "parallel"),
        name="attn_proj_conv_glu_ffn",
    )(x, x, x, a, a, a, w_o, gattn, gpre, win_c, cw_c, cb_c, wout_c, gpost)


def _norm_qkv_da_kernel(x_ref, g_ref, wqT_ref, wk_ref, wvT_ref, cos_ref, sin_ref, cosT_ref, sinT_ref,
                        qT_ref, k_ref, vT_ref, *, q_scale):
    xn = _rms(x_ref[...], g_ref[...]).astype(BF16)
    nt = (((1,), (1,)), ((), ()))
    half = DA_HEAD_DIM // 2

    kk = jnp.dot(xn, wk_ref[...], preferred_element_type=F32)
    cos = cos_ref[...]
    sin = sin_ref[...]
    lane = lax.broadcasted_iota(jnp.int32, cos.shape, 1)
    low = (lane % DA_HEAD_DIM) < half
    for c in range(D_MODEL // 128):
        xc = kk[:, c * 128:(c + 1) * 128]
        rot = jnp.where(low, pltpu.roll(xc, 128 - half, 1), pltpu.roll(xc, half, 1))
        k_ref[c] = (xc * cos + rot * sin).astype(BF16)

    qT = lax.dot_general(wqT_ref[...], xn, nt, preferred_element_type=F32)
    cl, ch = cosT_ref[0:half], cosT_ref[half:DA_HEAD_DIM]
    sl, sh = sinT_ref[0:half], sinT_ref[half:DA_HEAD_DIM]
    for grp in range(D_MODEL // DA_HEAD_DIM):
        base = grp * DA_HEAD_DIM
        lo = qT[base:base + half]
        hi = qT[base + half:base + DA_HEAD_DIM]
        qT_ref[base:base + half] = ((lo * cl - hi * sl) * q_scale).astype(BF16)
        qT_ref[base + half:base + DA_HEAD_DIM] = ((hi * ch + lo * sh) * q_scale).astype(BF16)

    vT = lax.dot_general(wvT_ref[...], xn, nt, preferred_element_type=F32)
    for hd in range(DA_HEADS):
        vT_ref[hd] = vT[hd * 2 * DA_HEAD_DIM:(hd + 1) * 2 * DA_HEAD_DIM].astype(BF16)


def _norm_qkv_da(x, g, wqT, wk, wvT, tables):
    b, t, _ = x.shape
    tm = DA_K_TILE
    nk = t // tm
    cos128, sin128, cosT, sinT = tables
    hv = 2 * DA_HEAD_DIM
    return pl.pallas_call(
        functools.partial(_norm_qkv_da_kernel, q_scale=(DA_HEAD_DIM ** -0.5) * LOG2E),
        out_shape=(
            jax.ShapeDtypeStruct((b, D_MODEL, t), BF16),
            jax.ShapeDtypeStruct((b, DA_HEADS, t, hv), BF16),
            jax.ShapeDtypeStruct((b, DA_HEADS, nk, hv, tm), BF16),
        ),
        grid=(b, nk),
        in_specs=[
            pl.BlockSpec((None, tm, D_MODEL), lambda bi, i: (bi, i, 0)),
            _resident((1, D_MODEL)),
            _resident((D_MODEL, D_MODEL)),
            _resident((D_MODEL, D_MODEL)),
            _resident((D_MODEL, D_MODEL)),
            pl.BlockSpec((tm, 128), lambda bi, i: (i, 0)),
            pl.BlockSpec((tm, 128), lambda bi, i: (i, 0)),
            pl.BlockSpec((DA_HEAD_DIM, tm), lambda bi, i: (0, i)),
            pl.BlockSpec((DA_HEAD_DIM, tm), lambda bi, i: (0, i)),
        ],
        out_specs=(
            pl.BlockSpec((None, D_MODEL, tm), lambda bi, i: (bi, 0, i)),
            pl.BlockSpec((None, DA_HEADS, tm, hv), lambda bi, i: (bi, 0, i, 0)),
            pl.BlockSpec((None, DA_HEADS, None, hv, tm), lambda bi, i: (bi, 0, i, 0, 0)),
        ),
        compiler_params=_params("parallel", "parallel"),
        name="norm_qkv_da",
    )(x, g, wqT, wk, wvT, cos128, sin128, cosT, sinT)


def _rotary_tables(t):
    half = DA_HEAD_DIM // 2
    inv = 1.0 / (ROPE_THETA ** (jnp.arange(0, DA_HEAD_DIM, 2, dtype=F32) / DA_HEAD_DIM))
    ang = jnp.arange(t, dtype=F32)[:, None] * inv[None, :]
    ang = jnp.concatenate([ang, ang], axis=-1)
    cos, sin = jnp.cos(ang), jnp.sin(ang)
    sign = jnp.where(jnp.arange(DA_HEAD_DIM) < half, -1.0, 1.0).astype(F32)
    cos128 = jnp.concatenate([cos, cos], axis=-1)
    sin128 = jnp.concatenate([sin * sign, sin * sign], axis=-1)
    return cos128, sin128, cos.T, sin.T


def _da_kernel(qT_ref, qT_next_ref, k_ref, vT_ref, lam_ref, g_ref, o_ref,
               qz_ref, qz_next_ref, st_a, st_b, mx_a, mx_b, m_ref, acc_ref, *, tq, tk, nk, lambda_init):
    hd = DA_HEAD_DIM
    hv = 2 * DA_HEAD_DIM
    zero = jnp.zeros((hd, tq), BF16)

    def build_qz(src_ref, dst_ref):
        qT = src_ref[...]
        dst_ref[0:hd, 0:tq] = qT[0:hd]
        dst_ref[hd:hv, 0:tq] = zero
        dst_ref[0:hd, tq:2 * tq] = zero
        dst_ref[hd:hv, tq:2 * tq] = qT[hd:hv]

    def scores(j, zq_ref, st_ref, mx_ref):
        kj = k_ref[pl.ds(pl.multiple_of(j * tk, tk), tk), :]
        st = jnp.dot(kj, zq_ref[...], preferred_element_type=F32)
        st_ref[...] = st
        mx_ref[...] = jnp.max(st, axis=0, keepdims=True)

    ones = jnp.ones((DA_ONES_ROWS, tk), BF16)

    def accumulate(j, st_ref, mx_ref):
        m_old = m_ref[...]
        m_new = jnp.maximum(m_old, mx_ref[...])
        alpha = jnp.exp2(m_old - m_new)
        p = jnp.exp2(st_ref[...] - m_new).astype(BF16)
        vj = jnp.concatenate([vT_ref[j], ones], axis=0)
        acc_ref[...] = acc_ref[...] * alpha + jnp.dot(vj, p, preferred_element_type=F32)
        m_ref[...] = m_new

    bufs = ((st_a, mx_a), (st_b, mx_b))
    build_qz(qT_ref, qz_ref)
    m_ref[...] = jnp.full(m_ref.shape, MASK_VALUE, F32)
    acc_ref[...] = jnp.zeros_like(acc_ref)

    @pl.when(pl.program_id(2) == 0)
    def _():
        scores(0, qz_ref, *bufs[0])

    def stage_pair(j, parity):
        scores(j + 1, qz_ref, *bufs[1 - parity])
        accumulate(j, *bufs[parity])

    unroll = DA_STAGE_PAIRS_PER_BODY
    n_body = (nk - 1) // unroll

    def body(i, carry):
        for u in range(unroll):
            stage_pair(i * unroll + u, u % 2)
        return carry

    lax.fori_loop(0, n_body, body, 0)
    for j in range(n_body * unroll, nk - 1):
        stage_pair(j, j % 2)
    build_qz(qT_next_ref, qz_next_ref)
    scores(0, qz_next_ref, *bufs[0])
    accumulate(nk - 1, *bufs[1])

    acc = acc_ref[...]
    o1 = acc[0:hv, 0:tq] / acc[hv:hv + 1, 0:tq]
    o2 = acc[0:hv, tq:2 * tq] / acc[hv:hv + 1, tq:2 * tq]
    lv = lam_ref[...]
    lam = (jnp.exp(jnp.sum(lv[0:1] * lv[1:2], axis=-1, keepdims=True))
           - jnp.exp(jnp.sum(lv[2:3] * lv[3:4], axis=-1, keepdims=True)) + lambda_init)
    o = o1 - lam * o2
    ms = jnp.mean(o * o, axis=0, keepdims=True)
    y = o * lax.rsqrt(ms + DA_SUBLN_EPS) * g_ref[...] * (1.0 - lambda_init)
    o_ref[...] = y.T.astype(BF16)


def _da_attention(qT, k, vT, lam_vecs, subln_g_col, lambda_init):
    b, _, t, _ = k.shape
    tq, tk = DA_Q_TILE, DA_K_TILE
    nk = t // tk
    hv = 2 * DA_HEAD_DIM
    nq = t // tq
    assert nk >= 2 and nk % 2 == 0
    return pl.pallas_call(
        functools.partial(_da_kernel, tq=tq, tk=tk, nk=nk, lambda_init=lambda_init),
        out_shape=jax.ShapeDtypeStruct((b, DA_HEADS, t, hv), BF16),
        grid=(b, DA_HEADS, nq),
        in_specs=[
            pl.BlockSpec((None, hv, tq), lambda bi, h, qi: (bi, h, qi)),
            pl.BlockSpec((None, hv, tq), lambda bi, h, qi: (bi, h, jnp.minimum(qi + 1, nq - 1))),
            pl.BlockSpec((None, None, t, hv), lambda bi, h, qi: (bi, h, 0, 0)),
            pl.BlockSpec((None, None, nk, hv, tk), lambda bi, h, qi: (bi, h, 0, 0, 0)),
            _resident(lam_vecs.shape),
            _resident(subln_g_col.shape),
        ],
        out_specs=pl.BlockSpec((None, None, tq, hv), lambda bi, h, qi: (bi, h, qi, 0)),
        scratch_shapes=[
            pltpu.VMEM((hv, 2 * tq), BF16),
            pltpu.VMEM((hv, 2 * tq), BF16),
            pltpu.VMEM((tk, 2 * tq), F32),
            pltpu.VMEM((tk, 2 * tq), F32),
            pltpu.VMEM((1, 2 * tq), F32),
            pltpu.VMEM((1, 2 * tq), F32),
            pltpu.VMEM((1, 2 * tq), F32),
            pltpu.VMEM((hv + DA_ONES_ROWS, 2 * tq), F32),
        ],
        compiler_params=_params("parallel", "parallel", "arbitrary"),
        name="diff_attention",
    )(qT, qT, k, vT, lam_vecs, subln_g_col)


def _lambda_init(layer_idx):
    return 0.8 - 0.6 * math.exp(-0.3 * layer_idx)


def _trunk(x, p):
    b, t, d = x.shape
    n = b * t
    rot = _rotary_tables(t)
    for i in range(2):
        if i == 0:
            qkv = _norm_qkv_na(x.reshape(n, d), p["attn_pre_g"][i], p["na_w_qkv"])
            a = _na_attention(qkv.reshape(b, t, 3 * d), p["na_bias"])
            w_o = p["na_w_o"]
        else:
            qT, k, vT = _norm_qkv_da(x, p["attn_pre_g"][i], p["da_wqT"], p["da_wk"], p["da_wvT"], rot)
            a = _da_attention(qT, k, vT, p["da_lam"], p["da_subln_g"], _lambda_init(i))
            w_o = p["da_w_o"]
        x = _attn_proj_ffn(x, a, w_o, p["attn_post_g"][i], p["ffn_pre_g"][i], p["ffn_w_in"][i],
                           p["ffn_conv_w"][i], p["ffn_conv_b"][i], p["ffn_w_out"][i], p["ffn_post_g"][i])
    return x


def _prepare(attn_pre_g, attn_post_g, ffn_pre_g, ffn_post_g, na_w_qkv, na_rpb, na_w_o,
             da_w_q, da_w_k, da_w_v, da_lambda_q1, da_lambda_k1, da_lambda_q2, da_lambda_k2,
             da_subln_g, da_w_o, ffn_w_in, ffn_conv_w, ffn_conv_b, ffn_w_out):
    depth = ffn_w_in.shape[0]
    nc = D_FF // FFN_CHUNK

    def chunk_cols(w):
        lead = w.shape[:-1]
        w = w.reshape(lead + (2 * nc, FFN_CHUNK))
        return jnp.moveaxis(w, -2, 0)

    row = lambda v: v.reshape(v.shape[0], 1, v.shape[1]).astype(F32)
    return {
        "attn_pre_g": row(attn_pre_g), "attn_post_g": row(attn_post_g),
        "ffn_pre_g": row(ffn_pre_g), "ffn_post_g": row(ffn_post_g),
        "na_w_qkv": na_w_qkv[0].astype(BF16),
        "na_bias": _na_bias_table(na_rpb[0]),
        "na_w_o": na_w_o[0].astype(BF16),
        "da_wqT": da_w_q[0].T.astype(BF16),
        "da_wk": da_w_k[0].astype(BF16),
        "da_wvT": da_w_v[0].T.astype(BF16),
        "da_lam": jnp.stack([da_lambda_q1[0], da_lambda_k1[0], da_lambda_q2[0], da_lambda_k2[0]]).astype(F32),
        "da_subln_g": da_subln_g[0].astype(F32).reshape(2 * DA_HEAD_DIM, 1),
        "da_w_o": da_w_o[0].astype(BF16),
        "ffn_w_in": [chunk_cols(ffn_w_in[i]).astype(BF16) for i in range(depth)],
        "ffn_conv_w": [chunk_cols(ffn_conv_w[i]).astype(F32) for i in range(depth)],
        "ffn_conv_b": [chunk_cols(ffn_conv_b[i][None]).astype(F32) for i in range(depth)],
        "ffn_w_out": [ffn_w_out[i].reshape(nc, FFN_CHUNK, D_MODEL).astype(BF16) for i in range(depth)],
    }


def kernel(x_prompt, x_sample, attn_pre_g, attn_post_g, ffn_pre_g, ffn_post_g, na_w_qkv, na_rpb, na_w_o,
           da_w_q, da_w_k, da_w_v, da_lambda_q1, da_lambda_k1, da_lambda_q2, da_lambda_k2, da_subln_g, da_w_o,
           ffn_w_in, ffn_conv_w, ffn_conv_b, ffn_w_out):
    p = _prepare(attn_pre_g, attn_post_g, ffn_pre_g, ffn_post_g, na_w_qkv, na_rpb, na_w_o,
                 da_w_q, da_w_k, da_w_v, da_lambda_q1, da_lambda_k1, da_lambda_q2, da_lambda_k2,
                 da_subln_g, da_w_o, ffn_w_in, ffn_conv_w, ffn_conv_b, ffn_w_out)
    return (_trunk(x_prompt, p), _trunk(x_sample, p))
```
